```python
import jax
import jax.numpy as jnp
from jax import lax
import numpy as np

D_MODEL = 2048
BATCH = 1
SEQ = 8192
DEPTH = 2
DEC_BATCH = 128
DEC_SEQ = 8
PAST_LEN = 16384
PAGE_SIZE = 128

N_MIXERS = 2
N_SWA_LAYERS = (DEPTH + 1) // 2
N_MLA_LAYERS = DEPTH // 2

SWA_HEADS = 32
SWA_KV_HEADS = 4
SWA_HEAD_DIM = 64
SWA_GROUP = SWA_HEADS // SWA_KV_HEADS
WINDOW = 128
ALIBI_MAX_BIAS = 8.0

MLA_HEADS = 16
MLA_Q_LORA = 512
MLA_KV_LORA = 512
MLA_NOPE = 128
MLA_ROPE = 64
MLA_V = 128
MLA_QK = MLA_NOPE + MLA_ROPE
ROPE_THETA = 10000.0
Q_BLOCK = 128

N_EXPERTS = 32
TOP_K = 4
D_EXPERT = D_MODEL
SWIGLU_LIMIT = 7.0
SWIGLU_ALPHA = 1.702
EXPERT_BLOCK = 128

NORM_EPS = 1e-6

kernel_name = 'hybrid_swa_mla_moe_adaln_step'


def rmsnorm(x, g):
    xf = x.astype(jnp.float32)
    y = xf * lax.rsqrt(jnp.mean(xf * xf, axis=-1, keepdims=True) + NORM_EPS)
    return (y * g.astype(jnp.float32)).astype(x.dtype)


def adaln(c, w, b):
    mod = jax.nn.silu(c) @ w + b
    return jnp.split(mod[:, None, :], 6, axis=-1)


def modulate(x, g, shift, scale):
    return rmsnorm(x, g) * (1.0 + scale) + shift


def alibi_slopes():
    return jnp.exp2(-ALIBI_MAX_BIAS * jnp.arange(1, SWA_HEADS + 1, dtype=jnp.float32) / SWA_HEADS)


def swa_project(h, w_qkv, b_qkv, g_q, g_k):
    lead = h.shape[:-1]
    nq = SWA_HEADS * SWA_HEAD_DIM
    nk = SWA_KV_HEADS * SWA_HEAD_DIM
    qkv = h @ w_qkv + b_qkv
    q = qkv[..., :nq].reshape(*lead, SWA_KV_HEADS, SWA_GROUP, SWA_HEAD_DIM)
    k = qkv[..., nq:nq + nk].reshape(*lead, SWA_KV_HEADS, SWA_HEAD_DIM)
    v = qkv[..., nq + nk:].reshape(*lead, SWA_KV_HEADS, SWA_HEAD_DIM)
    return rmsnorm(q, g_q), rmsnorm(k, g_k), v


def swa_attend(q, k, v, qpos, kpos, sinks):
    s = jnp.einsum('...qkgd,...skd->...kgqs', q, k).astype(jnp.float32) * (SWA_HEAD_DIM ** -0.5)
    dist = qpos[..., :, None] - kpos[..., None, :]
    valid = (dist >= 0) & (dist < WINDOW) & (kpos[..., None, :] >= 0)
    slopes = alibi_slopes().reshape(SWA_KV_HEADS, SWA_GROUP, 1, 1)
    s = jnp.where(valid[..., None, None, :, :],
                  s - slopes * dist[..., None, None, :, :].astype(jnp.float32), -jnp.inf)
    sink = sinks.astype(jnp.float32).reshape(SWA_KV_HEADS, SWA_GROUP, 1)
    m = jnp.maximum(s.max(axis=-1), sink)
    p = jnp.exp(s - m[..., None])
    p = p / (p.sum(axis=-1) + jnp.exp(sink - m))[..., None]
    o = jnp.einsum('...kgqs,...skd->...qkgd', p.astype(v.dtype), v)
    return o.reshape(*o.shape[:-3], SWA_HEADS * SWA_HEAD_DIM)


def swa_prompt(h, w_qkv, b_qkv, g_q, g_k, sinks, w_o, b_o):
    B, S, _ = h.shape
    q, k, v = swa_project(h, w_qkv, b_qkv, g_q, g_k)
    nb = S // WINDOW
    qb = q.reshape(B, nb, WINDOW, SWA_KV_HEADS, SWA_GROUP, SWA_HEAD_DIM)
    kb = k.reshape(B, nb, WINDOW, SWA_KV_HEADS, SWA_HEAD_DIM)
    vb = v.reshape(B, nb, WINDOW, SWA_KV_HEADS, SWA_HEAD_DIM)

    def with_prev(t):
        prev = jnp.concatenate([jnp.zeros_like(t[:, :1]), t[:, :-1]], axis=1)
        return jnp.concatenate([prev, t], axis=2)

    pos = jnp.arange(S, dtype=jnp.int32).reshape(nb, WINDOW)
    kpos = jnp.concatenate([pos - WINDOW, pos], axis=1)
    o = swa_attend(qb, with_prev(kb), with_prev(vb), pos, kpos, sinks).reshape(B, S, -1)
    w = min(WINDOW, S)
    return o @ w_o + b_o, k[:, S - w:], v[:, S - w:]


def swa_sample(h, buf_k, buf_v, past_len, w_qkv, b_qkv, g_q, g_k, sinks, w_o, b_o):
    B, Q, _ = h.shape
    nbuf = buf_k.shape[1]
    q, k, v = swa_project(h, w_qkv, b_qkv, g_q, g_k)
    k_all = jnp.concatenate([buf_k, k], axis=1)
    v_all = jnp.concatenate([buf_v, v], axis=1)
    qpos = past_len + jnp.arange(Q, dtype=jnp.int32)
    kpos = jnp.concatenate([past_len - nbuf + jnp.arange(nbuf, dtype=jnp.int32), qpos])
    o = swa_attend(q, k_all, v_all, qpos, kpos, sinks)
    return o @ w_o + b_o, k_all[:, -nbuf:], v_all[:, -nbuf:]


def rope_tail(x, pos):
    half = MLA_ROPE // 2
    inv = ROPE_THETA ** (-jnp.arange(half, dtype=jnp.float32) / half)
    ang = pos.astype(jnp.float32)[:, None] * inv
    cos = jnp.cos(ang)[:, None, :]
    sin = jnp.sin(ang)[:, None, :]
    xp = x[..., :-MLA_ROPE]
    x1 = x[..., -MLA_ROPE:-half]
    x2 = x[..., -half:]
    return jnp.concatenate([xp, (x1 * cos - x2 * sin).astype(x.dtype), (x2 * cos + x1 * sin).astype(x.dtype)], axis=-1)


def mla_project(h, pos, w_dkv, g_qa, g_kva, w_uq, g_q):
    d = h @ w_dkv
    cq = rmsnorm(d[..., :MLA_Q_LORA], g_qa)
    ckv = rmsnorm(d[..., MLA_Q_LORA:MLA_Q_LORA + MLA_KV_LORA], g_kva)
    kr = d[..., MLA_Q_LORA + MLA_KV_LORA:]
    q = (cq @ w_uq).reshape(*h.shape[:-1], MLA_HEADS, MLA_QK)
    return rope_tail(rmsnorm(q, g_q), pos), ckv, kr


def mla_keys(ckv, kr, pos, w_uk, g_k):
    k_nope = (ckv @ w_uk).reshape(*ckv.shape[:-1], MLA_HEADS, MLA_NOPE)
    k_rope = jnp.broadcast_to(kr[..., None, :], (*kr.shape[:-1], MLA_HEADS, MLA_ROPE))
    k = jnp.concatenate([k_nope, k_rope], axis=-1)
    return rope_tail(rmsnorm(k, g_k), pos)


def mla_prompt(h, w_dkv, g_qa, g_kva, w_uq, g_q, w_uk, g_k, w_uv, w_o):
    B, S, _ = h.shape
    pos = jnp.arange(S, dtype=jnp.int32)
    q, ckv, kr = mla_project(h, pos, w_dkv, g_qa, g_kva, w_uq, g_q)
    k = mla_keys(ckv, kr, pos, w_uk, g_k)
    v = (ckv @ w_uv).reshape(B, S, MLA_HEADS, MLA_V)
    nqb = S // Q_BLOCK
    qb = jnp.moveaxis(q.reshape(B, nqb, Q_BLOCK, MLA_HEADS, MLA_QK), 1, 0)

    def block(args):
        qi, start = args
        s = jnp.einsum('bqhd,bkhd->bhqk', qi, k).astype(jnp.float32) * (MLA_QK ** -0.5)
        causal = pos[None, :] <= (start + jnp.arange(Q_BLOCK, dtype=jnp.int32))[:, None]
        p = jax.nn.softmax(jnp.where(causal, s, -jnp.inf), axis=-1)
        return jnp.einsum('bhqk,bkhv->bqhv', p.astype(v.dtype), v)

    o = lax.map(block, (qb, jnp.arange(nqb, dtype=jnp.int32) * Q_BLOCK))
    o = jnp.moveaxis(o, 0, 1).reshape(B, S, MLA_HEADS * MLA_V)
    return o @ w_o, ckv, kr


def mla_sample(h, pool_ckv, pool_kr, page_table, past_len, w_dkv, g_qa, g_kva, w_uq, g_q, w_uk, g_k, w_uv, w_o):
    B, Q, _ = h.shape
    page = pool_ckv.shape[1]
    n_pages = page_table.shape[1]
    qpos = past_len + jnp.arange(Q, dtype=jnp.int32)
    q, ckv, kr = mla_project(h, qpos, w_dkv, g_qa, g_kva, w_uq, g_q)

    def attend_block(carry, c_blk, kr_blk, kpos, mask):
        m, l, acc = carry
        kb = mla_keys(c_blk, kr_blk, kpos, w_uk, g_k)
        s = jnp.einsum('bqhd,bkhd->bhqk', q, kb).astype(jnp.float32) * (MLA_QK ** -0.5)
        if mask is not None:
            s = jnp.where(mask, s, -jnp.inf)
        m_new = jnp.maximum(m, s.max(axis=-1))
        p = jnp.exp(s - m_new[..., None])
        a = jnp.exp(m - m_new)
        acc = acc * a[..., None] + jnp.einsum('bhqk,bkc->bhqc', p, c_blk.astype(jnp.float32))
        return (m_new, l * a + p.sum(axis=-1), acc)

    def page_step(carry, xs):
        lp, phys = xs
        kpos = lp * page + jnp.arange(page, dtype=jnp.int32)
        return attend_block(carry, pool_ckv[phys], pool_kr[phys], kpos, None), None

    init = (jnp.full((B, MLA_HEADS, Q), -jnp.inf, jnp.float32),
            jnp.zeros((B, MLA_HEADS, Q), jnp.float32),
            jnp.zeros((B, MLA_HEADS, Q, MLA_KV_LORA), jnp.float32))
    carry, _ = lax.scan(page_step, init, (jnp.arange(n_pages, dtype=jnp.int32), page_table.T))
    causal = jnp.arange(Q)[None, :] <= jnp.arange(Q)[:, None]
    m, l, acc = attend_block(carry, ckv, kr, qpos, causal)
    o_lat = (acc / l[..., None]).astype(h.dtype)
    o = jnp.einsum('bhqc,chv->bqhv', o_lat, w_uv.reshape(MLA_KV_LORA, MLA_HEADS, MLA_V))
    return o.reshape(B, Q, MLA_HEADS * MLA_V) @ w_o, ckv, kr


def moe(h, w_r, b_r, w_gu, b_gu, w_d, b_d):
    shape = h.shape
    xt = h.reshape(-1, shape[-1])
    T = xt.shape[0]
    logits = (xt @ w_r + b_r).astype(jnp.float32)
    top_v, top_e = lax.top_k(logits, TOP_K)
    gates = jax.nn.softmax(top_v, axis=-1).reshape(-1)
    e_flat = top_e.reshape(-1)
    A = T * TOP_K
    order = jnp.argsort(e_flat)
    e_sorted = e_flat[order]
    counts = jnp.bincount(e_flat, length=N_EXPERTS)
    padded = (counts + EXPERT_BLOCK - 1) // EXPERT_BLOCK * EXPERT_BLOCK
    pad_end = jnp.cumsum(padded)
    slot = (pad_end - padded)[e_sorted] + jnp.arange(A, dtype=jnp.int32) - (jnp.cumsum(counts) - counts)[e_sorted]
    n_blocks = -(-(A + N_EXPERTS * (EXPERT_BLOCK - 1)) // EXPERT_BLOCK)
    n_slots = n_blocks * EXPERT_BLOCK
    tok = jnp.zeros((n_slots,), jnp.int32).at[slot].set((order // TOP_K).astype(jnp.int32))
    gate = jnp.zeros((n_slots,), jnp.float32).at[slot].set(gates[order])
    blk_e = jnp.minimum(jnp.searchsorted(pad_end, jnp.arange(n_blocks, dtype=jnp.int32) * EXPERT_BLOCK, side='right'), N_EXPERTS - 1)

    def expert_block(args):
        e, t, g = args
        xb = xt[t]
        gu = (xb @ w_gu[e] + b_gu[e]).astype(jnp.float32)
        gl = jnp.minimum(gu[:, :D_EXPERT], SWIGLU_LIMIT)
        up = jnp.clip(gu[:, D_EXPERT:], -SWIGLU_LIMIT, SWIGLU_LIMIT)
        act = (up + 1.0) * gl * jax.nn.sigmoid(SWIGLU_ALPHA * gl)
        return (act.astype(xt.dtype) @ w_d[e] + b_d[e]).astype(jnp.float32) * g[:, None]

    out = lax.map(expert_block, (blk_e, tok.reshape(n_blocks, EXPERT_BLOCK), gate.reshape(n_blocks, EXPERT_BLOCK)))
    y = jnp.zeros((T, shape[-1]), jnp.float32).at[tok].add(out.reshape(n_slots, shape[-1]))
    return y.astype(h.dtype).reshape(shape)


def setup_inputs(seed: int = 0) -> dict:
    key = jax.random.key(seed)
    ks = iter(jax.random.split(key, 48))

    def nrm(shape, scale=1.0):
        return jax.random.normal(next(ks), shape, jnp.float32) * scale

    def gain(shape):
        return 1.0 + 0.05 * nrm(shape)

    D = D_MODEL
    n_pages = PAST_LEN // PAGE_SIZE
    n_pool = (5 * DEC_BATCH * n_pages + 3) // 4
    w_buf = min(WINDOW, PAST_LEN)
    perm = jax.random.permutation(next(ks), n_pool)[:DEC_BATCH * n_pages]
    page_table = perm.reshape(DEC_BATCH, n_pages).astype(jnp.int32)
    n_qkv = (SWA_HEADS + 2 * SWA_KV_HEADS) * SWA_HEAD_DIM
    return {
        'x_prompt': nrm((BATCH, SEQ, D)),
        'x_sample': nrm((DEC_BATCH, DEC_SEQ, D)),
        'state_swa_k': nrm((N_SWA_LAYERS, DEC_BATCH, w_buf, SWA_KV_HEADS, SWA_HEAD_DIM)),
        'state_swa_v': nrm((N_SWA_LAYERS, DEC_BATCH, w_buf, SWA_KV_HEADS, SWA_HEAD_DIM)),
        'cache_mla_ckv': nrm((N_MLA_LAYERS, n_pool, PAGE_SIZE, MLA_KV_LORA)),
        'cache_mla_kr': nrm((N_MLA_LAYERS, n_pool, PAGE_SIZE, MLA_ROPE)),
        'page_table': page_table,
        'c_prompt': nrm((BATCH, D)),
        'c_sample': nrm((DEC_BATCH, D)),
        'ada_w': nrm((DEPTH, D, 6 * D), 0.5 * D ** -0.5),
        'ada_b': nrm((DEPTH, 6 * D), 0.01),
        'norm_attn_g': gain((DEPTH, D)),
        'norm_ffn_g': gain((DEPTH, D)),
        'swa_w_qkv': nrm((N_SWA_LAYERS, D, n_qkv), D ** -0.5),
        'swa_b_qkv': nrm((N_SWA_LAYERS, n_qkv), 0.01),
        'swa_g_q': gain((N_SWA_LAYERS, SWA_HEAD_DIM)),
        'swa_g_k': gain((N_SWA_LAYERS, SWA_HEAD_DIM)),
        'swa_sinks': nrm((N_SWA_LAYERS, SWA_HEADS), 0.5),
        'swa_w_o': nrm((N_SWA_LAYERS, SWA_HEADS * SWA_HEAD_DIM, D), (SWA_HEADS * SWA_HEAD_DIM) ** -0.5),
        'swa_b_o': nrm((N_SWA_LAYERS, D), 0.01),
        'mla_w_dkv': nrm((N_MLA_LAYERS, D, MLA_Q_LORA + MLA_KV_LORA + MLA_ROPE), D ** -0.5),
        'mla_g_qa': gain((N_MLA_LAYERS, MLA_Q_LORA)),
        'mla_g_kva': gain((N_MLA_LAYERS, MLA_KV_LORA)),
        'mla_w_uq': nrm((N_MLA_LAYERS, MLA_Q_LORA, MLA_HEADS * MLA_QK), MLA_Q_LORA ** -0.5),
        'mla_g_q': gain((N_MLA_LAYERS, MLA_QK)),
        'mla_w_uk': nrm((N_MLA_LAYERS, MLA_KV_LORA, MLA_HEADS * MLA_NOPE), MLA_KV_LORA ** -0.5),
        'mla_g_k': gain((N_MLA_LAYERS, MLA_QK)),
        'mla_w_uv': nrm((N_MLA_LAYERS, MLA_KV_LORA, MLA_HEADS * MLA_V), MLA_KV_LORA ** -0.5),
        'mla_w_o': nrm((N_MLA_LAYERS, MLA_HEADS * MLA_V, D), (MLA_HEADS * MLA_V) ** -0.5),
        'moe_w_router': nrm((DEPTH, D, N_EXPERTS), D ** -0.5),
        'moe_b_router': nrm((DEPTH, N_EXPERTS), 0.01),
        'moe_w_gate_up': nrm((DEPTH, N_EXPERTS, D, 2 * D_EXPERT), D ** -0.5),
        'moe_b_gate_up': nrm((DEPTH, N_EXPERTS, 2 * D_EXPERT), 0.01),
        'moe_w_down': nrm((DEPTH, N_EXPERTS, D_EXPERT, D), D_EXPERT ** -0.5),
        'moe_b_down': nrm((DEPTH, N_EXPERTS, D), 0.01),
    }


def reference(x_prompt, x_sample, state_swa_k, state_swa_v, cache_mla_ckv, cache_mla_kr, page_table,
              c_prompt, c_sample, ada_w, ada_b, norm_attn_g, norm_ffn_g,
              swa_w_qkv, swa_b_qkv, swa_g_q, swa_g_k, swa_sinks, swa_w_o, swa_b_o,
              mla_w_dkv, mla_g_qa, mla_g_kva, mla_w_uq, mla_g_q, mla_w_uk, mla_g_k, mla_w_uv, mla_w_o,
              moe_w_router, moe_b_router, moe_w_gate_up, moe_b_gate_up, moe_w_down, moe_b_down):
    past_len = page_table.shape[1] * cache_mla_ckv.shape[2]
    yp, ys = x_prompt, x_sample
    swa_kp, swa_vp, swa_ks, swa_vs = [], [], [], []
    ckv_p, kr_p, ckv_s, kr_s = [], [], [], []
    for i in range(DEPTH):
        sh1p, sc1p, g1p, sh2p, sc2p, g2p = adaln(c_prompt, ada_w[i], ada_b[i])
        sh1s, sc1s, g1s, sh2s, sc2s, g2s = adaln(c_sample, ada_w[i], ada_b[i])
        hp = modulate(yp, norm_attn_g[i], sh1p, sc1p)
        hs = modulate(ys, norm_attn_g[i], sh1s, sc1s)
        j = i // N_MIXERS
        if i % N_MIXERS == 0:
            swa = (swa_w_qkv[j], swa_b_qkv[j], swa_g_q[j], swa_g_k[j], swa_sinks[j], swa_w_o[j], swa_b_o[j])
            op, kp, vp = swa_prompt(hp, *swa)
            o_s, k_s, v_s = swa_sample(hs, state_swa_k[j], state_swa_v[j], past_len, *swa)
            swa_kp.append(kp)
            swa_vp.append(vp)
            swa_ks.append(k_s)
            swa_vs.append(v_s)
        else:
            mla = (mla_w_dkv[j], mla_g_qa[j], mla_g_kva[j], mla_w_uq[j], mla_g_q[j],
                   mla_w_uk[j], mla_g_k[j], mla_w_uv[j], mla_w_o[j])
            op, cp, rp = mla_prompt(hp, *mla)
            o_s, c_s, r_s = mla_sample(hs, cache_mla_ckv[j], cache_mla_kr[j], page_table, past_len, *mla)
            ckv_p.append(cp)
            kr_p.append(rp)
            ckv_s.append(c_s)
            kr_s.append(r_s)
        yp = yp + g1p * op
        ys = ys + g1s * o_s
        ffn = (moe_w_router[i], moe_b_router[i], moe_w_gate_up[i], moe_b_gate_up[i], moe_w_down[i], moe_b_down[i])
        yp = yp + g2p * moe(modulate(yp, norm_ffn_g[i], sh2p, sc2p), *ffn)
        ys = ys + g2s * moe(modulate(ys, norm_ffn_g[i], sh2s, sc2s), *ffn)
    return (yp, ys, jnp.stack(swa_kp), jnp.stack(swa_vp), jnp.stack(swa_ks), jnp.stack(swa_vs),
            jnp.stack(ckv_p), jnp.stack(kr_p), jnp.stack(ckv_s), jnp.stack(kr_s))
```

```python
import functools

import jax
import jax.numpy as jnp
from jax import lax
from jax.experimental import pallas as pl
from jax.experimental.pallas import tpu as pltpu

F32 = jnp.float32
BF16 = jnp.bfloat16

WINDOW = 128
ALIBI_MAX_BIAS = 8.0
ROPE_THETA = 10000.0
TOP_K = 4
SWIGLU_LIMIT = 7.0
SWIGLU_ALPHA = 1.702
NORM_EPS = 1e-6

MASKED = -1e30
VMEM_LIMIT_BYTES = 56 * 1024 * 1024
MXU_WIDTH = 256
EXPERT_ROWS = 512
PAGES_PER_STEP = 8


def _pick(n, prefs):
    for p in prefs:
        if n % p == 0:
            return p
    raise ValueError(f"no tile in {prefs} divides {n}")


def _dot(a, b):
    return jnp.dot(a, b, preferred_element_type=F32)


def _dot_nt(a, b):
    return lax.dot_general(a, b, (((1,), (1,)), ((), ())), preferred_element_type=F32)


def _split_bf16(x):
    hi = x.astype(BF16)
    lo = (x - hi.astype(F32)).astype(BF16)
    return hi, lo


def _params(n_axes):
    return pltpu.CompilerParams(dimension_semantics=("arbitrary",) * n_axes,
                                vmem_limit_bytes=VMEM_LIMIT_BYTES)


def _modulate(x, g, scale, shift):
    y = x * lax.rsqrt(jnp.mean(x * x, axis=-1, keepdims=True) + NORM_EPS) * g
    return y * (1.0 + scale) + shift


def _adaln_body(c_ref, w_ref, b_ref, o_ref):
    c = c_ref[...]
    a = (c * jax.nn.sigmoid(c)).astype(BF16)
    o_ref[...] = _dot(a, w_ref[...].astype(BF16)) + b_ref[...]


def _adaln(c_all, ada_w, ada_b):
    depth, d, n = ada_w.shape
    m = c_all.shape[0]
    tn = _pick(n, (1024, 512, 256, 128))
    return pl.pallas_call(
        _adaln_body,
        grid=(depth, n // tn),
        in_specs=[pl.BlockSpec((m, d), lambda l, j: (0, 0)),
                  pl.BlockSpec((None, d, tn), lambda l, j: (l, 0, j)),
                  pl.BlockSpec((None, 1, tn), lambda l, j: (l, 0, j))],
        out_specs=pl.BlockSpec((None, m, tn), lambda l, j: (l, 0, j)),
        out_shape=jax.ShapeDtypeStruct((depth, m, n), F32),
        compiler_params=_params(2),
        name="adaln",
    )(c_all, ada_w, ada_b.reshape(depth, 1, n))


class _Geom:
    def __init__(self, b, s, db, q, d):
        self.b, self.s, self.db, self.q, self.d = b, s, db, q, d
        self.tp, self.ts = b * s, db * q
        self.t = self.tp + self.ts
        rb = 256
        while s % rb or self.ts % rb or rb % q:
            rb //= 2
            if rb < 8:
                raise ValueError("token counts must be multiples of 8")
        self.rb = rb
        self.nbp, self.nbs = self.tp // rb, self.ts // rb
        self.nb = self.nbp + self.nbs

    def row(self, width, offset=0):
        return pl.BlockSpec((self.rb, width), lambda i: (i + offset, 0))

    def full(self, arr):
        nd = arr.ndim
        return pl.BlockSpec(arr.shape, lambda i: (0,) * nd)

    def mod(self, chunk, offset=0):
        nbp, spb, b = self.nbp, self.s // self.rb, self.b

        def index(i):
            i = i + offset
            return (jnp.where(i < nbp, i // spb, b + i - nbp), chunk)

        return pl.BlockSpec((self.rb, self.d), index)

    def expand_mod(self, mod):
        return jnp.concatenate([jnp.repeat(mod[:self.b], self.rb, axis=0),
                                jnp.repeat(mod[self.b:self.b + self.db], self.q, axis=0)], axis=0)


def _swa_proj_body(x_ref, g_ref, sc_ref, sh_ref, w_ref, b_ref, gq_ref, gk_ref, e_ref,
                   q_ref, k_ref, v_ref, *, nq, nk, hd):
    h = _modulate(x_ref[...], g_ref[...], sc_ref[...], sh_ref[...]).astype(BF16)
    qkv = _dot(h, w_ref[...]) + b_ref[...]
    e = e_ref[...]
    inv_hd = 1.0 / hd
    for c in range(nq // nk):
        qc = qkv[:, c * nk:(c + 1) * nk]
        ss = _dot((qc * qc).astype(BF16), e)
        q_ref[:, c * nk:(c + 1) * nk] = (qc * lax.rsqrt(ss * inv_hd + NORM_EPS) * gq_ref[...]).astype(BF16)
    kc = qkv[:, nq:nq + nk]
    ss = _dot((kc * kc).astype(BF16), e)
    k_ref[...] = kc * lax.rsqrt(ss * inv_hd + NORM_EPS) * gk_ref[...]
    v_ref[...] = qkv[:, nq + nk:]


def _swa_prompt_body(sink_ref, q_ref, kc_ref, kp_ref, vc_ref, vp_ref, bias_ref, o_ref, *, kv, grp, hd):
    i = pl.program_id(0)
    w = q_ref.shape[0]
    col = lax.broadcasted_iota(jnp.int32, (1, 2 * w), 1)
    no_prev = jnp.where(jnp.logical_and(i == 0, col < w), MASKED, 0.0)
    for g in range(kv):
        ks = slice(g * hd, (g + 1) * hd)
        kcat = jnp.concatenate([kp_ref[:, ks], kc_ref[:, ks]], axis=0).astype(BF16)
        vcat = jnp.concatenate([vp_ref[:, ks], vc_ref[:, ks]], axis=0).astype(BF16)
        qg = jnp.concatenate([q_ref[:, (g * grp + j) * hd:(g * grp + j + 1) * hd] for j in range(grp)], axis=0)
        s = _dot_nt(qg, kcat) + bias_ref[g] + no_prev
        sink = jnp.concatenate([jnp.full((w, 1), sink_ref[g * grp + j], F32) for j in range(grp)], axis=0)
        m = jnp.maximum(jnp.max(s, axis=-1, keepdims=True), sink)
        p = jnp.exp(s - m)
        den = jnp.sum(p, axis=-1, keepdims=True) + jnp.exp(sink - m)
        og = _dot((p * (1.0 / den)).astype(BF16), vcat)
        for j in range(grp):
            o_ref[:, (g * grp + j) * hd:(g * grp + j + 1) * hd] = og[j * w:(j + 1) * w, :].astype(BF16)


def _swa_sample_body(sink_ref, q_ref, kn_ref, vn_ref, ks_ref, vs_ref, bias_old_ref, bias_new_ref,
                     o_ref, ko_ref, vo_ref, *, kv, grp, hd, nq_tok):
    nb, nbuf, _ = ks_ref.shape
    qt = nq_tok
    qf = q_ref[...].astype(F32)
    pad = jnp.zeros((nbuf - qt, hd), F32)
    for b in range(nb):
        rows = slice(b * qt, (b + 1) * qt)
        knew, vnew = kn_ref[rows, :], vn_ref[rows, :]
        ko_ref[b, :nbuf - qt, :] = ks_ref[b, qt:, :]
        ko_ref[b, nbuf - qt:, :] = knew
        vo_ref[b, :nbuf - qt, :] = vs_ref[b, qt:, :]
        vo_ref[b, nbuf - qt:, :] = vnew
        for g in range(kv):
            cs = slice(g * hd, (g + 1) * hd)
            k_old = ks_ref[b, :, cs].astype(BF16)
            v_old = vs_ref[b, :, cs].astype(BF16)
            k_new = jnp.concatenate([knew[:, cs], pad], axis=0).astype(BF16)
            v_new = jnp.concatenate([vnew[:, cs], pad], axis=0).astype(BF16)
            qg = jnp.concatenate([qf[rows, (g * grp + j) * hd:(g * grp + j + 1) * hd] for j in range(grp)],
                                 axis=0).astype(BF16)
            s_old = _dot_nt(qg, k_old) + bias_old_ref[g]
            s_new = _dot_nt(qg, k_new) + bias_new_ref[g]
            sink = jnp.concatenate([jnp.full((qt, 1), sink_ref[g * grp + j], F32) for j in range(grp)], axis=0)
            m = jnp.maximum(jnp.maximum(jnp.max(s_old, axis=-1, keepdims=True),
                                        jnp.max(s_new, axis=-1, keepdims=True)), sink)
            p_old = jnp.exp(s_old - m)
            p_new = jnp.exp(s_new - m)
            den = (jnp.sum(p_old, axis=-1, keepdims=True) + jnp.sum(p_new, axis=-1, keepdims=True)
                   + jnp.exp(sink - m))
            inv = 1.0 / den
            og = _dot((p_old * inv).astype(BF16), v_old) + _dot((p_new * inv).astype(BF16), v_new)
            for j in range(grp):
                o_ref[rows, (g * grp + j) * hd:(g * grp + j + 1) * hd] = og[j * qt:(j + 1) * qt, :]


def _alibi_bias(dist, valid, heads, kv):
    slopes = jnp.exp2(-ALIBI_MAX_BIAS * jnp.arange(1, heads + 1, dtype=F32) / heads)
    bias = jnp.where(valid[None], -slopes[:, None, None] * dist[None].astype(F32), MASKED)
    return bias.reshape(kv, (heads // kv) * dist.shape[0], dist.shape[1])


def _swa_layer(geom, x, modx, norm_g, w_qkv, b_qkv, g_q, g_k, sinks, w_o, b_o, state_k, state_v, past_len):
    d, rb = geom.d, geom.rb
    heads = sinks.shape[0]
    hd = g_q.shape[0]
    kv = state_k.shape[-2]
    grp = heads // kv
    nq, nk = heads * hd, kv * hd
    nbuf = state_k.shape[1]
    qt = geom.q
    if WINDOW != nbuf or geom.s % WINDOW:
        raise ValueError("window buffer must hold exactly one window and tile the prompt")

    head_of = jnp.arange(nk) // hd
    e = (head_of[:, None] == head_of[None, :]).astype(BF16)
    gq_t = (jnp.tile(g_q, kv) * hd ** -0.5).reshape(1, nk)
    gk_t = jnp.tile(g_k, kv).reshape(1, nk)

    q, k, v = pl.pallas_call(
        functools.partial(_swa_proj_body, nq=nq, nk=nk, hd=hd),
        grid=(geom.nb,),
        in_specs=[geom.row(d), geom.full(norm_g), geom.mod(1), geom.mod(0),
                  pl.BlockSpec(w_qkv.shape, lambda i: (0, 0)), pl.BlockSpec((1, nq + 2 * nk), lambda i: (0, 0)),
                  geom.full(gq_t), geom.full(gk_t), geom.full(e)],
        out_specs=[geom.row(nq), geom.row(nk), geom.row(nk)],
        out_shape=[jax.ShapeDtypeStruct((geom.t, nq), BF16), jax.ShapeDtypeStruct((geom.t, nk), F32),
                   jax.ShapeDtypeStruct((geom.t, nk), F32)],
        compiler_params=_params(1),
        name="swa_proj",
    )(x, norm_g, modx, modx, w_qkv.astype(BF16), b_qkv.reshape(1, -1), gq_t, gk_t, e)

    w = WINDOW
    t_idx = jnp.arange(w)[:, None]
    s_idx = jnp.arange(2 * w)[None, :]
    dist = t_idx + w - s_idx
    bias_p = _alibi_bias(dist, (dist >= 0) & (dist < WINDOW), heads, kv)
    nblk = geom.tp // w
    bpb = geom.s // w

    def prev_block(i):
        return (jnp.where(i % bpb == 0, i, i - 1), 0)

    o_p = pl.pallas_call(
        functools.partial(_swa_prompt_body, kv=kv, grp=grp, hd=hd),
        grid=(nblk,),
        in_specs=[pl.BlockSpec(memory_space=pltpu.SMEM),
                  pl.BlockSpec((w, nq), lambda i: (i, 0)),
                  pl.BlockSpec((w, nk), lambda i: (i, 0)), pl.BlockSpec((w, nk), prev_block),
                  pl.BlockSpec((w, nk), lambda i: (i, 0)), pl.BlockSpec((w, nk), prev_block),
                  pl.BlockSpec(bias_p.shape, lambda i: (0, 0, 0))],
        out_specs=pl.BlockSpec((w, nq), lambda i: (i, 0)),
        out_shape=jax.ShapeDtypeStruct((geom.tp, nq), BF16),
        compiler_params=_params(1),
        name="swa_prompt_attn",
    )(sinks, q, k, k, v, v, bias_p)

    qi = jnp.arange(qt)[:, None]
    so = jnp.arange(nbuf)[None, :]
    dist_old = qi + nbuf - so
    bias_old = _alibi_bias(dist_old, (dist_old < WINDOW) & (past_len - nbuf + so >= 0), heads, kv)
    dist_new = qi - so
    bias_new = _alibi_bias(dist_new, (dist_new >= 0) & (so < qt), heads, kv)
    nbt = _pick(geom.db, (8, 4, 2, 1))
    rows = nbt * qt
    off = geom.tp // rows
    o_s, k_s, v_s = pl.pallas_call(
        functools.partial(_swa_sample_body, kv=kv, grp=grp, hd=hd, nq_tok=qt),
        grid=(geom.db // nbt,),
        in_specs=[pl.BlockSpec(memory_space=pltpu.SMEM),
                  pl.BlockSpec((rows, nq), lambda i: (i + off, 0)),
                  pl.BlockSpec((rows, nk), lambda i: (i + off, 0)), pl.BlockSpec((rows, nk), lambda i: (i + off, 0)),
                  pl.BlockSpec((nbt, nbuf, nk), lambda i: (i, 0, 0)), pl.BlockSpec((nbt, nbuf, nk), lambda i: (i, 0, 0)),
                  pl.BlockSpec(bias_old.shape, lambda i: (0, 0, 0)), pl.BlockSpec(bias_new.shape, lambda i: (0, 0, 0))],
        out_specs=[pl.BlockSpec((rows, nq), lambda i: (i, 0)),
                   pl.BlockSpec((nbt, nbuf, nk), lambda i: (i, 0, 0)), pl.BlockSpec((nbt, nbuf, nk), lambda i: (i, 0, 0))],
        out_shape=[jax.ShapeDtypeStruct((geom.ts, nq), F32),
                   jax.ShapeDtypeStruct((geom.db, nbuf, nk), F32), jax.ShapeDtypeStruct((geom.db, nbuf, nk), F32)],
        compiler_params=_params(1),
        name="swa_sample_attn",
    )(sinks, q, k, v, state_k.reshape(geom.db, nbuf, nk), state_v.reshape(geom.db, nbuf, nk), bias_old, bias_new)

    o_all = jnp.concatenate([o_p, o_s.astype(BF16)], axis=0)
    x = _out_proj(geom, o_all, w_o, b_o, x, modx, 2)

    wbuf = min(WINDOW, geom.s)
    kp = k[:geom.tp].reshape(geom.b, geom.s, kv, hd)[:, geom.s - wbuf:]
    vp = v[:geom.tp].reshape(geom.b, geom.s, kv, hd)[:, geom.s - wbuf:]
    return x, kp, vp, k_s.reshape(geom.db, nbuf, kv, hd), v_s.reshape(geom.db, nbuf, kv, hd)


def _out_proj_body(o_ref, w_ref, b_ref, x_ref, gate_ref, out_ref):
    y = _dot(o_ref[...], w_ref[...]) + b_ref[...]
    out_ref[...] = x_ref[...] + gate_ref[...] * y


def _out_proj_nobias_body(o_ref, w_ref, x_ref, gate_ref, out_ref):
    out_ref[...] = x_ref[...] + gate_ref[...] * _dot(o_ref[...], w_ref[...])


def _out_proj(geom, o_all, w_o, b_o, x, modx, gate_chunk):
    d = geom.d
    w_bf = w_o.astype(BF16)
    if b_o is None:
        body, extra, extra_specs = _out_proj_nobias_body, (), []
    else:
        body, extra, extra_specs = _out_proj_body, (b_o.reshape(1, d),), [pl.BlockSpec((1, d), lambda i: (0, 0))]
    return pl.pallas_call(
        body,
        grid=(geom.nb,),
        in_specs=[geom.row(o_all.shape[1]), geom.full(w_bf)] + extra_specs + [geom.row(d), geom.mod(gate_chunk)],
        out_specs=geom.row(d),
        out_shape=jax.ShapeDtypeStruct((geom.t, d), F32),
        compiler_params=_params(1),
        name="out_proj",
    )(o_all, w_bf, *extra, x, modx)


def _moe_prep_body(x_ref, g_ref, sc_ref, sh_ref, wr_hi_ref, wr_lo_ref, br_ref, h_ref, lg_ref):
    h = _modulate(x_ref[...], g_ref[...], sc_ref[...], sh_ref[...])
    h_hi, h_lo = _split_bf16(h)
    wr_hi = wr_hi_ref[...]
    lg_ref[...] = _dot(h_hi, wr_hi) + _dot(h_hi, wr_lo_ref[...]) + _dot(h_lo, wr_hi) + br_ref[...]
    h_ref[...] = h_hi


def _moe_expert_body(be_ref, nvalid_ref, x_ref, wg_ref, wu_ref, bg_ref, bu_ref, wd_ref, bd_ref, gate_ref, out_ref):
    blk, n = pl.program_id(0), pl.program_id(1)
    last = pl.num_programs(1) - 1
    valid = blk < nvalid_ref[0]

    @pl.when(valid)
    def _():
        x = x_ref[...]
        gl = jnp.minimum(_dot(x, wg_ref[...].astype(BF16)) + bg_ref[...], SWIGLU_LIMIT)
        up = jnp.clip(_dot(x, wu_ref[...].astype(BF16)) + bu_ref[...], -SWIGLU_LIMIT, SWIGLU_LIMIT)
        act = (up + 1.0) * gl * jax.nn.sigmoid(SWIGLU_ALPHA * gl)
        part = _dot(act.astype(BF16), wd_ref[...].astype(BF16))

        @pl.when(n == 0)
        def _():
            out_ref[...] = part + bd_ref[...]

        @pl.when(n > 0)
        def _():
            out_ref[...] += part

        @pl.when(n == last)
        def _():
            out_ref[...] *= gate_ref[...]

    @pl.when(jnp.logical_and(jnp.logical_not(valid), n == 0))
    def _():
        out_ref[...] = jnp.zeros_like(out_ref)


def _moe_layer(geom, layer, x, modx, norm_g, w_r, b_r, w_gu, b_gu, w_d, b_d):
    d, t = geom.d, geom.t
    n_exp = w_r.shape[-1]
    de = w_d.shape[2]
    lanes = 128
    e_pad = -(-n_exp // lanes) * lanes
    wr_hi, wr_lo = _split_bf16(jnp.pad(w_r, ((0, 0), (0, e_pad - n_exp))))
    br = jnp.pad(b_r, (0, e_pad - n_exp)).reshape(1, e_pad)

    h, logits = pl.pallas_call(
        _moe_prep_body,
        grid=(geom.nb,),
        in_specs=[geom.row(d), geom.full(norm_g), geom.mod(4), geom.mod(3),
                  geom.full(wr_hi), geom.full(wr_lo), geom.full(br)],
        out_specs=[geom.row(d), geom.row(e_pad)],
        out_shape=[jax.ShapeDtypeStruct((t, d), BF16), jax.ShapeDtypeStruct((t, e_pad), F32)],
        compiler_params=_params(1),
        name="moe_prep",
    )(x, norm_g, modx, modx, wr_hi, wr_lo, br)

    tm = EXPERT_ROWS
    top_v, top_e = lax.top_k(logits[:, :n_exp], TOP_K)
    gates = jax.nn.softmax(top_v, axis=-1).reshape(-1)
    e_flat = top_e.reshape(-1).astype(jnp.int32)
    n_assign = t * TOP_K
    order = jnp.argsort(e_flat)
    e_sorted = e_flat[order]
    counts = jnp.bincount(e_flat, length=n_exp).astype(jnp.int32)
    padded = (counts + tm - 1) // tm * tm
    pad_end = jnp.cumsum(padded)
    slot = (pad_end - padded)[e_sorted] + jnp.arange(n_assign, dtype=jnp.int32) - (jnp.cumsum(counts) - counts)[e_sorted]
    n_blocks = -(-(n_assign + n_exp * (tm - 1)) // tm)
    n_slots = n_blocks * tm
    tok = jnp.zeros((n_slots,), jnp.int32).at[slot].set((order // TOP_K).astype(jnp.int32))
    gate = jnp.zeros((n_slots,), F32).at[slot].set(gates[order])
    blk_e = jnp.minimum(jnp.searchsorted(pad_end, jnp.arange(n_blocks, dtype=jnp.int32) * tm, side='right'),
                        n_exp - 1).astype(jnp.int32)
    n_valid = (pad_end[-1] // tm).astype(jnp.int32).reshape(1)
    slot_of = jnp.zeros((n_assign,), jnp.int32).at[order].set(slot.astype(jnp.int32)).reshape(t, TOP_K)

    x_sorted = jnp.take(h, tok, axis=0)

    tn = _pick(de, (512, 256, 128))
    nt = de // tn
    out = pl.pallas_call(
        _moe_expert_body,
        grid_spec=pltpu.PrefetchScalarGridSpec(
            num_scalar_prefetch=2,
            grid=(n_blocks, nt),
            in_specs=[pl.BlockSpec((tm, d), lambda b, n, be, nv: (b, 0)),
                      pl.BlockSpec((None, None, d, tn), lambda b, n, be, nv: (layer, be[b], 0, n)),
                      pl.BlockSpec((None, None, d, tn), lambda b, n, be, nv: (layer, be[b], 0, nt + n)),
                      pl.BlockSpec((None, None, 1, tn), lambda b, n, be, nv: (layer, be[b], 0, n)),
                      pl.BlockSpec((None, None, 1, tn), lambda b, n, be, nv: (layer, be[b], 0, nt + n)),
                      pl.BlockSpec((None, None, tn, d), lambda b, n, be, nv: (layer, be[b], n, 0)),
                      pl.BlockSpec((None, None, 1, d), lambda b, n, be, nv: (layer, be[b], 0, 0)),
                      pl.BlockSpec((tm, 1), lambda b, n, be, nv: (b, 0))],
            out_specs=pl.BlockSpec((tm, d), lambda b, n, be, nv: (b, 0))),
        out_shape=jax.ShapeDtypeStruct((n_slots, d), F32),
        compiler_params=_params(2),
        name="moe_experts",
    )(blk_e, n_valid, x_sorted, w_gu, w_gu, b_gu.reshape(*b_gu.shape[:2], 1, -1), b_gu.reshape(*b_gu.shape[:2], 1, -1),
      w_d, b_d.reshape(*b_d.shape[:2], 1, -1), gate.reshape(n_slots, 1))

    y = jnp.take(out, slot_of, axis=0).sum(axis=1)
    return _gated_residual(geom, x, y, modx, 5)


def _gated_residual_body(x_ref, y_ref, gate_ref, out_ref):
    out_ref[...] = x_ref[...] + gate_ref[...] * y_ref[...]


def _gated_residual(geom, x, y, modx, gate_chunk):
    d = geom.d
    return pl.pallas_call(
        _gated_residual_body,
        grid=(geom.nb,),
        in_specs=[geom.row(d), geom.row(d), geom.mod(gate_chunk)],
        out_specs=geom.row(d),
        out_shape=jax.ShapeDtypeStruct((geom.t, d), F32),
        compiler_params=_params(1),
        name="gated_residual",
    )(x, y, modx)


def _rope_tables(pos, half):
    inv = ROPE_THETA ** (-jnp.arange(half, dtype=F32) / half)
    ang = pos.astype(F32)[:, None] * inv
    return jnp.cos(ang), jnp.sin(ang)


def _mla_down_body(x_ref, g_ref, sc_ref, sh_ref, w_ref, gqa_ref, gkva_ref,
                   cq_ref, ckv_ref, kr_ref, krs_ref, *, ql, kl, rope):
    h = _modulate(x_ref[...], g_ref[...], sc_ref[...], sh_ref[...]).astype(BF16)
    dd = _dot(h, w_ref[...])
    dq = dd[:, :ql]
    cq_ref[...] = (dq * lax.rsqrt(jnp.mean(dq * dq, axis=-1, keepdims=True) + NORM_EPS) * gqa_ref[...]).astype(BF16)
    dk = dd[:, ql:ql + kl]
    ckv_ref[...] = dk * lax.rsqrt(jnp.mean(dk * dk, axis=-1, keepdims=True) + NORM_EPS) * gkva_ref[...]
    kr_ref[...] = dd[:, ql + kl:ql + kl + rope]
    krs_ref[...] = dd[:, ql + kl + rope:]


def _mla_q_prompt_body(cq_ref, w_ref, a_ref, b_ref, q_ref, *, heads, hw, qk):
    cq = cq_ref[...]
    a, bt = a_ref[...], b_ref[...]
    for h in range(heads):
        x = _dot(cq, w_ref[:, h * hw:(h + 1) * hw])
        xs = _dot(cq, w_ref[:, (heads + h) * hw:(heads + h + 1) * hw])
        r = lax.rsqrt(jnp.sum(x * x, axis=-1, keepdims=True) * (1.0 / qk) + NORM_EPS)
        q_ref[:, h * hw:(h + 1) * hw] = ((x * a + xs * bt) * r).astype(BF16)


def _mla_q_sample_body(cq_ref, w_ref, a_ref, b_ref, a2_ref, b2_ref, wuk_ref, qabs_ref, qr1_ref, qr2_ref,
                       *, heads, hw, qk, nope, rope, nq_tok):
    cq = cq_ref[...]
    rb = cq.shape[0]
    nb = rb // nq_tok
    a, bt = a_ref[...], b_ref[...]
    a2, b2 = a2_ref[...], b2_ref[...]
    for h in range(heads):
        x = _dot(cq, w_ref[:, h * hw:(h + 1) * hw])
        xs = _dot(cq, w_ref[:, (heads + h) * hw:(heads + h + 1) * hw])
        r = lax.rsqrt(jnp.sum(x * x, axis=-1, keepdims=True) * (1.0 / qk) + NORM_EPS)
        qn = (x[:, :nope] * a[:, :nope] * r).astype(BF16)
        qabs = _dot_nt(qn, wuk_ref[:, h * nope:(h + 1) * nope])
        xr, xsr = x[:, nope:nope + rope], xs[:, nope:nope + rope]
        q1 = (xr * a[:, nope:nope + rope] + xsr * bt[:, nope:nope + rope]) * r
        q2 = (xsr * a2 + xr * b2) * r
        rows = slice(h * nq_tok, (h + 1) * nq_tok)
        qabs_ref[:, rows, :] = qabs.reshape(nb, nq_tok, qabs.shape[1])
        qr1_ref[:, rows, :] = q1.reshape(nb, nq_tok, rope)
        qr2_ref[:, rows, :] = q2.reshape(nb, nq_tok, rope)


def _mla_kv_prompt_body(ckv_ref, kr_ref, krs_ref, wuk_ref, wuv_ref, gkn_ref, ak_ref, bk_ref, k_ref, v_ref,
                        *, heads, hw, qk, nope, rope, vd):
    c = ckv_ref[...].astype(BF16)
    kr = kr_ref[...]
    ssr = jnp.sum(kr * kr, axis=-1, keepdims=True)
    krot = kr * ak_ref[...] + krs_ref[...] * bk_ref[...]
    gkn = gkn_ref[...]
    zeros = jnp.zeros((c.shape[0], hw - nope - rope), BF16)
    for h in range(heads):
        kn = _dot(c, wuk_ref[:, h * nope:(h + 1) * nope])
        r = lax.rsqrt((jnp.sum(kn * kn, axis=-1, keepdims=True) + ssr) * (1.0 / qk) + NORM_EPS)
        k_ref[:, h * hw:h * hw + nope] = (kn * gkn * r).astype(BF16)
        k_ref[:, h * hw + nope:h * hw + nope + rope] = (krot * r).astype(BF16)
        k_ref[:, h * hw + nope + rope:(h + 1) * hw] = zeros
        v_ref[:, h * vd:(h + 1) * vd] = _dot(c, wuv_ref[:, h * vd:(h + 1) * vd]).astype(BF16)


def _flash_body(qi_ref, kj_ref, q_ref, k_ref, v_ref, o_ref, m_ref, l_ref, acc_ref):
    n = pl.program_id(1)
    qi, kj = qi_ref[n], kj_ref[n]
    tq, tk = q_ref.shape[0], k_ref.shape[0]

    @pl.when(kj == 0)
    def _():
        m_ref[...] = jnp.full_like(m_ref, MASKED)
        l_ref[...] = jnp.zeros_like(l_ref)
        acc_ref[...] = jnp.zeros_like(acc_ref)

    s = _dot_nt(q_ref[...], k_ref[...])
    rows = qi * tq + lax.broadcasted_iota(jnp.int32, (tq, 1), 0)
    cols = kj * tk + lax.broadcasted_iota(jnp.int32, (1, tk), 1)
    s = jnp.where(cols <= rows, s, MASKED)
    m_old = m_ref[...]
    m_new = jnp.maximum(m_old, jnp.max(s, axis=-1, keepdims=True))
    p = jnp.exp(s - m_new)
    alpha = jnp.exp(m_old - m_new)
    l_ref[...] = l_ref[...] * alpha + jnp.sum(p, axis=-1, keepdims=True)
    acc_ref[...] = acc_ref[...] * alpha + _dot(p.astype(BF16), v_ref[...])
    m_ref[...] = m_new

    @pl.when(kj == qi)
    def _():
        o_ref[...] = (acc_ref[...] * (1.0 / l_ref[...])).astype(BF16)


def _mla_sample_body(pt_ref, qabs_ref, qr1_ref, qr2_ref, cnew_ref, krnew_ref, *rest,
                     heads, nq_tok, nope, qk, pages):
    c_refs, kr_refs = rest[:pages], rest[pages:2 * pages]
    cg_ref, sg_ref, cgn_ref, sgn_ref, wukt_ref, out_ref, qabs_bf, m_ref, l_ref, acc_ref = rest[2 * pages:]
    j = pl.program_id(1)
    hq = heads * nq_tok

    @pl.when(j == 0)
    def _():
        m_ref[...] = jnp.full_like(m_ref, MASKED)
        l_ref[...] = jnp.zeros_like(l_ref)
        acc_ref[...] = jnp.zeros_like(acc_ref)
        qabs_bf[...] = qabs_ref[...].astype(BF16)

    q1 = qr1_ref[...].astype(BF16)
    q2 = qr2_ref[...].astype(BF16)

    def attend(c_bf, kr, cg, sg, mask):
        n = c_bf.shape[0]
        kt = _dot_nt(wukt_ref[...], c_bf)
        ssn = jnp.sum((kt * kt).reshape(heads, nope, n), axis=1)
        kr2_hi, kr2_lo = _split_bf16(kr * kr)
        ones = jnp.ones((8, kr.shape[1]), BF16)
        ssr = (_dot_nt(ones, kr2_hi) + _dot_nt(ones, kr2_lo))[0:1, :]
        rk = lax.rsqrt((ssn + ssr) * (1.0 / qk) + NORM_EPS)
        s = (_dot_nt(qabs_bf[...], c_bf) + _dot_nt(q1, (kr * cg).astype(BF16))
             + _dot_nt(q2, (kr * sg).astype(BF16)))
        s = s * jnp.broadcast_to(rk[:, None, :], (heads, nq_tok, n)).reshape(hq, n)
        if mask is not None:
            s = jnp.where(mask, s, MASKED)
        m_old = m_ref[...]
        m_new = jnp.maximum(m_old, jnp.max(s, axis=-1, keepdims=True))
        p = jnp.exp(s - m_new)
        alpha = jnp.exp(m_old - m_new)
        l_ref[...] = l_ref[...] * alpha + jnp.sum(p, axis=-1, keepdims=True)
        acc_ref[...] = acc_ref[...] * alpha + _dot(p.astype(BF16), c_bf)
        m_ref[...] = m_new

    c_bf = jnp.concatenate([r[...].astype(BF16) for r in c_refs], axis=0)
    kr = jnp.concatenate([r[...] for r in kr_refs], axis=0)
    attend(c_bf, kr, cg_ref[...], sg_ref[...], None)

    @pl.when(j == pl.num_programs(1) - 1)
    def _():
        n_new = cnew_ref.shape[0]
        row_q = lax.broadcasted_iota(jnp.int32, (hq, n_new), 0) % nq_tok
        col = lax.broadcasted_iota(jnp.int32, (hq, n_new), 1)
        attend(cnew_ref[...].astype(BF16), krnew_ref[...], cgn_ref[...], sgn_ref[...], col <= row_q)
        out_ref[...] = acc_ref[...] * (1.0 / l_ref[...])


def _mla_uv_body(lat_ref, wuv_ref, o_ref, *, heads, nq_tok, vd):
    nb, _, kl = lat_ref.shape
    for h in range(heads):
        lat = lat_ref[:, h * nq_tok:(h + 1) * nq_tok, :].reshape(nb * nq_tok, kl).astype(BF16)
        o_ref[:, h * vd:(h + 1) * vd] = _dot(lat, wuv_ref[:, h * vd:(h + 1) * vd])


def _mla_layer(geom, layer, x, modx, norm_g, w_dkv, g_qa, g_kva, w_uq, g_q, w_uk, g_k, w_uv, w_o,
               cache_ckv, cache_kr, page_table, past_len):
    d, rb, qt = geom.d, geom.rb, geom.q
    ql, kl = g_qa.shape[0], g_kva.shape[0]
    rope = cache_kr.shape[-1]
    half = rope // 2
    qk = g_q.shape[0]
    nope = qk - rope
    heads = w_uq.shape[1] // qk
    vd = w_uv.shape[1] // heads
    hw = MXU_WIDTH
    scale = qk ** -0.5
    page = cache_ckv.shape[2]
    n_pages = page_table.shape[1]
    tp, ts = geom.tp, geom.ts

    w_dn = jnp.concatenate([w_dkv, w_dkv[:, ql + kl + half:], w_dkv[:, ql + kl:ql + kl + half]], axis=1).astype(BF16)
    gqa, gkva = g_qa.reshape(1, ql), g_kva.reshape(1, kl)
    cq, ckv, kr, krs = pl.pallas_call(
        functools.partial(_mla_down_body, ql=ql, kl=kl, rope=rope),
        grid=(geom.nb,),
        in_specs=[geom.row(d), geom.full(norm_g), geom.mod(1), geom.mod(0), geom.full(w_dn),
                  geom.full(gqa), geom.full(gkva)],
        out_specs=[geom.row(ql), geom.row(kl), geom.row(rope), geom.row(rope)],
        out_shape=[jax.ShapeDtypeStruct((geom.t, ql), BF16), jax.ShapeDtypeStruct((geom.t, kl), F32),
                   jax.ShapeDtypeStruct((geom.t, rope), F32), jax.ShapeDtypeStruct((geom.t, rope), F32)],
        compiler_params=_params(1),
        name="mla_down",
    )(x, norm_g, modx, modx, w_dn, gqa, gkva)

    w3 = w_uq.reshape(ql, heads, qk)
    wn, w1, w2 = w3[:, :, :nope], w3[:, :, nope:nope + half], w3[:, :, nope + half:]
    zpad = jnp.zeros((ql, heads, hw - qk), F32)
    w_q = jnp.concatenate([jnp.concatenate([wn, w1, w2, zpad], axis=-1).reshape(ql, heads * hw),
                           jnp.concatenate([jnp.zeros_like(wn), w2, w1, zpad], axis=-1).reshape(ql, heads * hw)],
                          axis=1).astype(BF16)
    gqn, gq1, gq2 = g_q[:nope], g_q[nope:nope + half], g_q[nope + half:]
    gkn, gk1, gk2 = g_k[:nope], g_k[nope:nope + half], g_k[nope + half:]

    def q_tables(pos, nope_gain):
        cos, sin = _rope_tables(pos, half)
        n = pos.shape[0]
        z = jnp.zeros((n, hw - qk), F32)
        a = jnp.concatenate([jnp.broadcast_to(nope_gain, (n, nope)), gq1 * cos, gq2 * cos, z], axis=1) * scale
        b = jnp.concatenate([jnp.zeros((n, nope), F32), -gq2 * sin, gq1 * sin, z], axis=1) * scale
        return a, b, cos, sin

    pos_p = jnp.tile(jnp.arange(geom.s, dtype=jnp.int32), geom.b)
    pos_s = jnp.tile(past_len + jnp.arange(qt, dtype=jnp.int32), geom.db)
    a_p, b_p, cos_p, sin_p = q_tables(pos_p, gqn)
    a_s, b_s, cos_s, sin_s = q_tables(pos_s, gqn * gkn)
    a2_s = jnp.concatenate([gq2 * cos_s, -gq1 * cos_s], axis=1) * scale
    b2_s = jnp.concatenate([gq1 * sin_s, gq2 * sin_s], axis=1) * scale

    q_p = pl.pallas_call(
        functools.partial(_mla_q_prompt_body, heads=heads, hw=hw, qk=qk),
        grid=(geom.nbp,),
        in_specs=[geom.row(ql), geom.full(w_q), geom.row(hw), geom.row(hw)],
        out_specs=geom.row(heads * hw),
        out_shape=jax.ShapeDtypeStruct((tp, heads * hw), BF16),
        compiler_params=_params(1),
        name="mla_q_prompt",
    )(cq, w_q, a_p, b_p)

    w_uk_bf = w_uk.astype(BF16)
    nbq = rb // qt
    hq = heads * qt
    qabs, qr1, qr2 = pl.pallas_call(
        functools.partial(_mla_q_sample_body, heads=heads, hw=hw, qk=qk, nope=nope, rope=rope, nq_tok=qt),
        grid=(geom.nbs,),
        in_specs=[geom.row(ql, geom.nbp), geom.full(w_q), geom.row(hw), geom.row(hw), geom.row(rope), geom.row(rope),
                  geom.full(w_uk_bf)],
        out_specs=[pl.BlockSpec((nbq, hq, kl), lambda i: (i, 0, 0)), pl.BlockSpec((nbq, hq, rope), lambda i: (i, 0, 0)),
                   pl.BlockSpec((nbq, hq, rope), lambda i: (i, 0, 0))],
        out_shape=[jax.ShapeDtypeStruct((geom.db, hq, kl), F32), jax.ShapeDtypeStruct((geom.db, hq, rope), F32),
                   jax.ShapeDtypeStruct((geom.db, hq, rope), F32)],
        compiler_params=_params(1),
        name="mla_q_sample",
    )(cq, w_q, a_s, b_s, a2_s, b2_s, w_uk_bf)

    w_uv_bf = w_uv.astype(BF16)
    a_k = jnp.concatenate([gk1 * cos_p, gk2 * cos_p], axis=1)
    b_k = jnp.concatenate([-gk2 * sin_p, gk1 * sin_p], axis=1)
    gkn_row = gkn.reshape(1, nope)
    k_p, v_p = pl.pallas_call(
        functools.partial(_mla_kv_prompt_body, heads=heads, hw=hw, qk=qk, nope=nope, rope=rope, vd=vd),
        grid=(geom.nbp,),
        in_specs=[geom.row(kl), geom.row(rope), geom.row(rope), geom.full(w_uk_bf), geom.full(w_uv_bf),
                  geom.full(gkn_row), geom.row(rope), geom.row(rope)],
        out_specs=[geom.row(heads * hw), geom.row(heads * vd)],
        out_shape=[jax.ShapeDtypeStruct((tp, heads * hw), BF16), jax.ShapeDtypeStruct((tp, heads * vd), BF16)],
        compiler_params=_params(1),
        name="mla_kv_prompt",
    )(ckv, kr, krs, w_uk_bf, w_uv_bf, gkn_row, a_k, b_k)

    o_parts = []
    tq = _pick(geom.s, (1024, 512, 256, 128, 64))
    nqb = geom.s // tq
    pairs = [(i, j) for i in range(nqb) for j in range(i + 1)]
    qi_tab = jnp.asarray([p[0] for p in pairs], jnp.int32)
    kj_tab = jnp.asarray([p[1] for p in pairs], jnp.int32)
    for bi in range(geom.b):
        off = bi * nqb
        o_parts.append(pl.pallas_call(
            _flash_body,
            grid_spec=pltpu.PrefetchScalarGridSpec(
                num_scalar_prefetch=2,
                grid=(heads, len(pairs)),
                in_specs=[pl.BlockSpec((tq, hw), lambda h, n, qi, kj: (off + qi[n], h)),
                          pl.BlockSpec((tq, hw), lambda h, n, qi, kj: (off + kj[n], h)),
                          pl.BlockSpec((tq, vd), lambda h, n, qi, kj: (off + kj[n], h))],
                out_specs=pl.BlockSpec((tq, vd), lambda h, n, qi, kj: (qi[n], h)),
                scratch_shapes=[pltpu.VMEM((tq, 1), F32), pltpu.VMEM((tq, 1), F32), pltpu.VMEM((tq, vd), F32)]),
            out_shape=jax.ShapeDtypeStruct((geom.s, heads * vd), BF16),
            compiler_params=_params(2),
            name="mla_prompt_attn",
        )(qi_tab, kj_tab, q_p, k_p, v_p))

    pg = _pick(n_pages, (PAGES_PER_STEP, 4, 2, 1))
    n_new = 128
    ckv_s = ckv[tp:].reshape(geom.db, qt, kl)
    kr_s = kr[tp:].reshape(geom.db, qt, rope)
    cnew = jnp.pad(ckv_s, ((0, 0), (0, n_new - qt), (0, 0)))
    krnew = jnp.pad(kr_s, ((0, 0), (0, n_new - qt), (0, 0)))
    cos_c, sin_c = _rope_tables(jnp.arange(past_len + n_new, dtype=jnp.int32), half)
    cg = jnp.concatenate([gk1 * cos_c, gk2 * cos_c], axis=1)
    sg = jnp.concatenate([gk1 * sin_c, gk2 * sin_c], axis=1)
    cg_old, sg_old, cg_new, sg_new = cg[:past_len], sg[:past_len], cg[past_len:], sg[past_len:]
    w_uk_t = w_uk.T.astype(BF16)

    def page_spec(u, width):
        return pl.BlockSpec((None, None, page, width), lambda b, j, pt: (layer, pt[b, j * pg + u], 0, 0))

    o_lat = pl.pallas_call(
        functools.partial(_mla_sample_body, heads=heads, nq_tok=qt, nope=nope, qk=qk, pages=pg),
        grid_spec=pltpu.PrefetchScalarGridSpec(
            num_scalar_prefetch=1,
            grid=(geom.db, n_pages // pg),
            in_specs=[pl.BlockSpec((None, hq, kl), lambda b, j, pt: (b, 0, 0)),
                      pl.BlockSpec((None, hq, rope), lambda b, j, pt: (b, 0, 0)),
                      pl.BlockSpec((None, hq, rope), lambda b, j, pt: (b, 0, 0)),
                      pl.BlockSpec((None, n_new, kl), lambda b, j, pt: (b, 0, 0)),
                      pl.BlockSpec((None, n_new, rope), lambda b, j, pt: (b, 0, 0))]
            + [page_spec(u, kl) for u in range(pg)] + [page_spec(u, rope) for u in range(pg)]
            + [pl.BlockSpec((pg * page, rope), lambda b, j, pt: (j, 0)),
               pl.BlockSpec((pg * page, rope), lambda b, j, pt: (j, 0)),
               pl.BlockSpec((n_new, rope), lambda b, j, pt: (0, 0)),
               pl.BlockSpec((n_new, rope), lambda b, j, pt: (0, 0)),
               pl.BlockSpec(w_uk_t.shape, lambda b, j, pt: (0, 0))],
            out_specs=pl.BlockSpec((None, hq, kl), lambda b, j, pt: (b, 0, 0)),
            scratch_shapes=[pltpu.VMEM((hq, kl), BF16), pltpu.VMEM((hq, 1), F32), pltpu.VMEM((hq, 1), F32),
                            pltpu.VMEM((hq, kl), F32)]),
        out_shape=jax.ShapeDtypeStruct((geom.db, hq, kl), F32),
        compiler_params=_params(2),
        name="mla_sample_attn",
    )(page_table, qabs, qr1, qr2, cnew, krnew, *([cache_ckv] * pg), *([cache_kr] * pg),
      cg_old, sg_old, cg_new, sg_new, w_uk_t)

    o_s = pl.pallas_call(
        functools.partial(_mla_uv_body, heads=heads, nq_tok=qt, vd=vd),
        grid=(geom.nbs,),
        in_specs=[pl.BlockSpec((nbq, hq, kl), lambda i: (i, 0, 0)), geom.full(w_uv_bf)],
        out_specs=geom.row(heads * vd),
        out_shape=jax.ShapeDtypeStruct((ts, heads * vd), F32),
        compiler_params=_params(1),
        name="mla_uv",
    )(o_lat, w_uv_bf)

    o_all = jnp.concatenate(o_parts + [o_s.astype(BF16)], axis=0)
    x = _out_proj(geom, o_all, w_o, None, x, modx, 2)
    return (x, ckv[:tp].reshape(geom.b, geom.s, kl), kr[:tp].reshape(geom.b, geom.s, rope), ckv_s, kr_s)


def kernel(x_prompt, x_sample, state_swa_k, state_swa_v, cache_mla_ckv, cache_mla_kr, page_table, c_prompt, c_sample, ada_w, ada_b, norm_attn_g, norm_ffn_g, swa_w_qkv, swa_b_qkv, swa_g_q, swa_g_k, swa_sinks, swa_w_o, swa_b_o, mla_w_dkv, mla_g_qa, mla_g_kva, mla_w_uq, mla_g_q, mla_w_uk, mla_g_k, mla_w_uv, mla_w_o, moe_w_router, moe_b_router, moe_w_gate_up, moe_b_gate_up, moe_w_down, moe_b_down):
    b, s, d = x_prompt.shape
    db, qt, _ = x_sample.shape
    depth = ada_w.shape[0]
    past_len = page_table.shape[1] * cache_mla_ckv.shape[2]
    geom = _Geom(b, s, db, qt, d)

    n_c = b + db
    m_pad = -(-n_c // 8) * 8
    c_all = jnp.pad(jnp.concatenate([c_prompt, c_sample], axis=0), ((0, m_pad - n_c), (0, 0)))
    mod = _adaln(c_all, ada_w, ada_b)

    x = jnp.concatenate([x_prompt.reshape(b * s, d), x_sample.reshape(db * qt, d)], axis=0)
    swa_kp, swa_vp, swa_ks, swa_vs = [], [], [], []
    ckv_p, kr_p, ckv_s, kr_s = [], [], [], []
    n_mixers = 2
    for i in range(depth):
        modx = geom.expand_mod(mod[i])
        g_attn = norm_attn_g[i].reshape(1, d)
        g_ffn = norm_ffn_g[i].reshape(1, d)
        j = i // n_mixers
        if i % n_mixers == 0:
            x, kp, vp, ks, vs = _swa_layer(geom, x, modx, g_attn, swa_w_qkv[j], swa_b_qkv[j], swa_g_q[j], swa_g_k[j],
                                           swa_sinks[j], swa_w_o[j], swa_b_o[j], state_swa_k[j], state_swa_v[j], past_len)
            swa_kp.append(kp)
            swa_vp.append(vp)
            swa_ks.append(ks)
            swa_vs.append(vs)
        else:
            x, cp, rp, cs, rs = _mla_layer(geom, j, x, modx, g_attn, mla_w_dkv[j], mla_g_qa[j], mla_g_kva[j], mla_w_uq[j],
                                           mla_g_q[j], mla_w_uk[j], mla_g_k[j], mla_w_uv[j], mla_w_o[j],
                                           cache_mla_ckv, cache_mla_kr, page_table, past_len)
            ckv_p.append(cp)
            kr_p.append(rp)
            ckv_s.append(cs)
            kr_s.append(rs)
        x = _moe_layer(geom, i, x, modx, g_ffn, moe_w_router[i], moe_b_router[i], moe_w_gate_up, moe_b_gate_up,
                       moe_w_down, moe_b_down)

    y_prompt = x[:b * s].reshape(b, s, d)
    y_sample = x[b * s:].reshape(db, qt, d)
    return (y_prompt, y_sample, jnp.stack(swa_kp), jnp.stack(swa_vp), jnp.stack(swa_ks), jnp.stack(swa_vs),
            jnp.stack(ckv_p), jnp.stack(kr_p), jnp.stack(ckv_s), jnp.stack(kr_s))
```

```python
import functools

import jax
import jax.numpy as jnp
from jax import lax
from jax.experimental import pallas as pl
from jax.experimental.pallas import tpu as pltpu

F32 = jnp.float32
BF16 = jnp.bfloat16

WINDOW = 128
ALIBI_MAX_BIAS = 8.0
ROPE_THETA = 10000.0
TOP_K = 4
SWIGLU_LIMIT = 7.0
SWIGLU_ALPHA = 1.702
NORM_EPS = 1e-6

MASKED = -1e30
VMEM_LIMIT_BYTES = 56 * 1024 * 1024
MXU_WIDTH = 256
EXPERT_ROWS = 512
GATHER_UNROLL = 8
PAGES_PER_STEP = 32
SAMPLE_CHUNK_PAGES = 2
SAMPLE_HEAD_GROUP = 16


def _pick(n, prefs):
    for p in prefs:
        if n % p == 0:
            return p
    raise ValueError(f"no tile in {prefs} divides {n}")


def _dot(a, b):
    return jnp.dot(a, b, preferred_element_type=F32)


def _dot_nt(a, b):
    return lax.dot_general(a, b, (((1,), (1,)), ((), ())), preferred_element_type=F32)


def _split_bf16(x):
    hi = x.astype(BF16)
    lo = (x - hi.astype(F32)).astype(BF16)
    return hi, lo


def _params(n_axes, gathers=False):
    return pltpu.CompilerParams(dimension_semantics=("arbitrary",) * n_axes,
                                vmem_limit_bytes=VMEM_LIMIT_BYTES, disable_bounds_checks=gathers)


def _modulate(x, g, scale, shift):
    y = x * lax.rsqrt(jnp.mean(x * x, axis=-1, keepdims=True) + NORM_EPS) * g
    return y * (1.0 + scale) + shift


def _adaln_body(c_ref, w_ref, b_ref, o_ref):
    c = c_ref[...]
    a = (c * jax.nn.sigmoid(c)).astype(BF16)
    o_ref[...] = _dot(a, w_ref[...].astype(BF16)) + b_ref[...]


def _adaln(c_all, ada_w, ada_b):
    depth, d, n = ada_w.shape
    m = c_all.shape[0]
    tn = _pick(n, (1024, 512, 256, 128))
    return pl.pallas_call(
        _adaln_body,
        grid=(depth, n // tn),
        in_specs=[pl.BlockSpec((m, d), lambda l, j: (0, 0)),
                  pl.BlockSpec((None, d, tn), lambda l, j: (l, 0, j)),
                  pl.BlockSpec((None, 1, tn), lambda l, j: (l, 0, j))],
        out_specs=pl.BlockSpec((None, m, tn), lambda l, j: (l, 0, j)),
        out_shape=jax.ShapeDtypeStruct((depth, m, n), F32),
        compiler_params=_params(2),
        name="adaln",
    )(c_all, ada_w, ada_b.reshape(depth, 1, n))


class _Geom:
    def __init__(self, b, s, db, q, d):
        self.b, self.s, self.db, self.q, self.d = b, s, db, q, d
        self.tp, self.ts = b * s, db * q
        self.t = self.tp + self.ts
        rb = 256
        while s % rb or self.ts % rb or rb % q:
            rb //= 2
            if rb < 8:
                raise ValueError("token counts must be multiples of 8")
        self.rb = rb
        self.nbp, self.nbs = self.tp // rb, self.ts // rb
        self.nb = self.nbp + self.nbs

    def row(self, width, offset=0):
        return pl.BlockSpec((self.rb, width), lambda i: (i + offset, 0))

    def full(self, arr):
        nd = arr.ndim
        return pl.BlockSpec(arr.shape, lambda i: (0,) * nd)

    def mod(self, chunk, offset=0):
        nbp, spb, b = self.nbp, self.s // self.rb, self.b

        def index(i, *_):
            i = i + offset
            return (jnp.where(i < nbp, i // spb, b + i - nbp), chunk)

        return pl.BlockSpec((self.rb, self.d), index)

    def expand_mod(self, mod):
        return jnp.concatenate([jnp.repeat(mod[:self.b], self.rb, axis=0),
                                jnp.repeat(mod[self.b:self.b + self.db], self.q, axis=0)], axis=0)


def _swa_proj_body(x_ref, g_ref, sc_ref, sh_ref, w_ref, b_ref, gq_ref, gk_ref, e_ref,
                   q_ref, k_ref, v_ref, *, nq, nk, hd):
    h = _modulate(x_ref[...], g_ref[...], sc_ref[...], sh_ref[...]).astype(BF16)
    qkv = _dot(h, w_ref[...]) + b_ref[...]
    e = e_ref[...]
    inv_hd = 1.0 / hd
    for c in range(nq // nk):
        qc = qkv[:, c * nk:(c + 1) * nk]
        ss = _dot((qc * qc).astype(BF16), e)
        q_ref[:, c * nk:(c + 1) * nk] = (qc * lax.rsqrt(ss * inv_hd + NORM_EPS) * gq_ref[...]).astype(BF16)
    kc = qkv[:, nq:nq + nk]
    ss = _dot((kc * kc).astype(BF16), e)
    k_ref[...] = kc * lax.rsqrt(ss * inv_hd + NORM_EPS) * gk_ref[...]
    v_ref[...] = qkv[:, nq + nk:]


def _swa_prompt_body(sink_ref, q_ref, kc_ref, kp_ref, vc_ref, vp_ref, bias_ref, o_ref, *, kv, grp, hd):
    i = pl.program_id(0)
    w = q_ref.shape[0]
    col = lax.broadcasted_iota(jnp.int32, (1, 2 * w), 1)
    no_prev = jnp.where(jnp.logical_and(i == 0, col < w), MASKED, 0.0)
    for g in range(kv):
        ks = slice(g * hd, (g + 1) * hd)
        kcat = jnp.concatenate([kp_ref[:, ks], kc_ref[:, ks]], axis=0).astype(BF16)
        vcat = jnp.concatenate([vp_ref[:, ks], vc_ref[:, ks]], axis=0).astype(BF16)
        qg = jnp.concatenate([q_ref[:, (g * grp + j) * hd:(g * grp + j + 1) * hd] for j in range(grp)], axis=0)
        s = _dot_nt(qg, kcat) + bias_ref[g] + no_prev
        sink = jnp.concatenate([jnp.full((w, 1), sink_ref[g * grp + j], F32) for j in range(grp)], axis=0)
        m = jnp.maximum(jnp.max(s, axis=-1, keepdims=True), sink)
        p = jnp.exp(s - m)
        den = jnp.sum(p, axis=-1, keepdims=True) + jnp.exp(sink - m)
        og = _dot((p * (1.0 / den)).astype(BF16), vcat)
        for j in range(grp):
            o_ref[:, (g * grp + j) * hd:(g * grp + j + 1) * hd] = og[j * w:(j + 1) * w, :].astype(BF16)


def _swa_sample_body(sink_ref, q_ref, kn_ref, vn_ref, ks_ref, vs_ref, bias_old_ref, bias_new_ref,
                     o_ref, ko_ref, vo_ref, *, kv, grp, hd, nq_tok):
    nb, nbuf, _ = ks_ref.shape
    qt = nq_tok
    qf = q_ref[...].astype(F32)
    pad = jnp.zeros((nbuf - qt, hd), F32)
    for b in range(nb):
        rows = slice(b * qt, (b + 1) * qt)
        knew, vnew = kn_ref[rows, :], vn_ref[rows, :]
        ko_ref[b, :nbuf - qt, :] = ks_ref[b, qt:, :]
        ko_ref[b, nbuf - qt:, :] = knew
        vo_ref[b, :nbuf - qt, :] = vs_ref[b, qt:, :]
        vo_ref[b, nbuf - qt:, :] = vnew
        for g in range(kv):
            cs = slice(g * hd, (g + 1) * hd)
            k_old = ks_ref[b, :, cs].astype(BF16)
            v_old = vs_ref[b, :, cs].astype(BF16)
            k_new = jnp.concatenate([knew[:, cs], pad], axis=0).astype(BF16)
            v_new = jnp.concatenate([vnew[:, cs], pad], axis=0).astype(BF16)
            qg = jnp.concatenate([qf[rows, (g * grp + j) * hd:(g * grp + j + 1) * hd] for j in range(grp)],
                                 axis=0).astype(BF16)
            s_old = _dot_nt(qg, k_old) + bias_old_ref[g]
            s_new = _dot_nt(qg, k_new) + bias_new_ref[g]
            sink = jnp.concatenate([jnp.full((qt, 1), sink_ref[g * grp + j], F32) for j in range(grp)], axis=0)
            m = jnp.maximum(jnp.maximum(jnp.max(s_old, axis=-1, keepdims=True),
                                        jnp.max(s_new, axis=-1, keepdims=True)), sink)
            p_old = jnp.exp(s_old - m)
            p_new = jnp.exp(s_new - m)
            den = (jnp.sum(p_old, axis=-1, keepdims=True) + jnp.sum(p_new, axis=-1, keepdims=True)
                   + jnp.exp(sink - m))
            inv = 1.0 / den
            og = _dot((p_old * inv).astype(BF16), v_old) + _dot((p_new * inv).astype(BF16), v_new)
            for j in range(grp):
                o_ref[rows, (g * grp + j) * hd:(g * grp + j + 1) * hd] = og[j * qt:(j + 1) * qt, :]


def _alibi_bias(dist, valid, heads, kv):
    slopes = jnp.exp2(-ALIBI_MAX_BIAS * jnp.arange(1, heads + 1, dtype=F32) / heads)
    bias = jnp.where(valid[None], -slopes[:, None, None] * dist[None].astype(F32), MASKED)
    return bias.reshape(kv, (heads // kv) * dist.shape[0], dist.shape[1])


def _swa_layer(geom, x, modx, norm_g, w_qkv, b_qkv, g_q, g_k, sinks, w_o, b_o, state_k, state_v, past_len):
    d, rb = geom.d, geom.rb
    heads = sinks.shape[0]
    hd = g_q.shape[0]
    kv = state_k.shape[-2]
    grp = heads // kv
    nq, nk = heads * hd, kv * hd
    nbuf = state_k.shape[1]
    qt = geom.q
    if WINDOW != nbuf or geom.s % WINDOW:
        raise ValueError("window buffer must hold exactly one window and tile the prompt")

    head_of = jnp.arange(nk) // hd
    e = (head_of[:, None] == head_of[None, :]).astype(BF16)
    gq_t = (jnp.tile(g_q, kv) * hd ** -0.5).reshape(1, nk)
    gk_t = jnp.tile(g_k, kv).reshape(1, nk)

    q, k, v = pl.pallas_call(
        functools.partial(_swa_proj_body, nq=nq, nk=nk, hd=hd),
        grid=(geom.nb,),
        in_specs=[geom.row(d), geom.full(norm_g), geom.mod(1), geom.mod(0),
                  pl.BlockSpec(w_qkv.shape, lambda i: (0, 0)), pl.BlockSpec((1, nq + 2 * nk), lambda i: (0, 0)),
                  geom.full(gq_t), geom.full(gk_t), geom.full(e)],
        out_specs=[geom.row(nq), geom.row(nk), geom.row(nk)],
        out_shape=[jax.ShapeDtypeStruct((geom.t, nq), BF16), jax.ShapeDtypeStruct((geom.t, nk), F32),
                   jax.ShapeDtypeStruct((geom.t, nk), F32)],
        compiler_params=_params(1),
        name="swa_proj",
    )(x, norm_g, modx, modx, w_qkv.astype(BF16), b_qkv.reshape(1, -1), gq_t, gk_t, e)

    w = WINDOW
    t_idx = jnp.arange(w)[:, None]
    s_idx = jnp.arange(2 * w)[None, :]
    dist = t_idx + w - s_idx
    bias_p = _alibi_bias(dist, (dist >= 0) & (dist < WINDOW), heads, kv)
    nblk = geom.tp // w
    bpb = geom.s // w

    def prev_block(i):
        return (jnp.where(i % bpb == 0, i, i - 1), 0)

    o_p = pl.pallas_call(
        functools.partial(_swa_prompt_body, kv=kv, grp=grp, hd=hd),
        grid=(nblk,),
        in_specs=[pl.BlockSpec(memory_space=pltpu.SMEM),
                  pl.BlockSpec((w, nq), lambda i: (i, 0)),
                  pl.BlockSpec((w, nk), lambda i: (i, 0)), pl.BlockSpec((w, nk), prev_block),
                  pl.BlockSpec((w, nk), lambda i: (i, 0)), pl.BlockSpec((w, nk), prev_block),
                  pl.BlockSpec(bias_p.shape, lambda i: (0, 0, 0))],
        out_specs=pl.BlockSpec((w, nq), lambda i: (i, 0)),
        out_shape=jax.ShapeDtypeStruct((geom.tp, nq), BF16),
        compiler_params=_params(1),
        name="swa_prompt_attn",
    )(sinks, q, k, k, v, v, bias_p)

    qi = jnp.arange(qt)[:, None]
    so = jnp.arange(nbuf)[None, :]
    dist_old = qi + nbuf - so
    bias_old = _alibi_bias(dist_old, (dist_old < WINDOW) & (past_len - nbuf + so >= 0), heads, kv)
    dist_new = qi - so
    bias_new = _alibi_bias(dist_new, (dist_new >= 0) & (so < qt), heads, kv)
    nbt = _pick(geom.db, (8, 4, 2, 1))
    rows = nbt * qt
    off = geom.tp // rows
    o_s, k_s, v_s = pl.pallas_call(
        functools.partial(_swa_sample_body, kv=kv, grp=grp, hd=hd, nq_tok=qt),
        grid=(geom.db // nbt,),
        in_specs=[pl.BlockSpec(memory_space=pltpu.SMEM),
                  pl.BlockSpec((rows, nq), lambda i: (i + off, 0)),
                  pl.BlockSpec((rows, nk), lambda i: (i + off, 0)), pl.BlockSpec((rows, nk), lambda i: (i + off, 0)),
                  pl.BlockSpec((nbt, nbuf, nk), lambda i: (i, 0, 0)), pl.BlockSpec((nbt, nbuf, nk), lambda i: (i, 0, 0)),
                  pl.BlockSpec(bias_old.shape, lambda i: (0, 0, 0)), pl.BlockSpec(bias_new.shape, lambda i: (0, 0, 0))],
        out_specs=[pl.BlockSpec((rows, nq), lambda i: (i, 0)),
                   pl.BlockSpec((nbt, nbuf, nk), lambda i: (i, 0, 0)), pl.BlockSpec((nbt, nbuf, nk), lambda i: (i, 0, 0))],
        out_shape=[jax.ShapeDtypeStruct((geom.ts, nq), F32),
                   jax.ShapeDtypeStruct((geom.db, nbuf, nk), F32), jax.ShapeDtypeStruct((geom.db, nbuf, nk), F32)],
        compiler_params=_params(1),
        name="swa_sample_attn",
    )(sinks, q, k, v, state_k.reshape(geom.db, nbuf, nk), state_v.reshape(geom.db, nbuf, nk), bias_old, bias_new)

    o_all = jnp.concatenate([o_p, o_s.astype(BF16)], axis=0)
    x = _out_proj(geom, o_all, w_o, b_o, x, modx, 2)

    wbuf = min(WINDOW, geom.s)
    kp = k[:geom.tp].reshape(geom.b, geom.s, kv, hd)[:, geom.s - wbuf:]
    vp = v[:geom.tp].reshape(geom.b, geom.s, kv, hd)[:, geom.s - wbuf:]
    return x, kp, vp, k_s.reshape(geom.db, nbuf, kv, hd), v_s.reshape(geom.db, nbuf, kv, hd)


def _out_proj_body(o_ref, w_ref, b_ref, x_ref, gate_ref, out_ref):
    y = _dot(o_ref[...], w_ref[...]) + b_ref[...]
    out_ref[...] = x_ref[...] + gate_ref[...] * y


def _out_proj_nobias_body(o_ref, w_ref, x_ref, gate_ref, out_ref):
    out_ref[...] = x_ref[...] + gate_ref[...] * _dot(o_ref[...], w_ref[...])


def _out_proj(geom, o_all, w_o, b_o, x, modx, gate_chunk):
    d = geom.d
    w_bf = w_o.astype(BF16)
    if b_o is None:
        body, extra, extra_specs = _out_proj_nobias_body, (), []
    else:
        body, extra, extra_specs = _out_proj_body, (b_o.reshape(1, d),), [pl.BlockSpec((1, d), lambda i: (0, 0))]
    return pl.pallas_call(
        body,
        grid=(geom.nb,),
        in_specs=[geom.row(o_all.shape[1]), geom.full(w_bf)] + extra_specs + [geom.row(d), geom.mod(gate_chunk)],
        out_specs=geom.row(d),
        out_shape=jax.ShapeDtypeStruct((geom.t, d), F32),
        compiler_params=_params(1),
        name="out_proj",
    )(o_all, w_bf, *extra, x, modx)


def _moe_prep_body(x_ref, g_ref, sc_ref, sh_ref, wr_hi_ref, wr_lo_ref, br_ref, h_ref, lg_ref):
    h = _modulate(x_ref[...], g_ref[...], sc_ref[...], sh_ref[...])
    h_hi, h_lo = _split_bf16(h)
    wr_hi = wr_hi_ref[...]
    lg_ref[...] = _dot(h_hi, wr_hi) + _dot(h_hi, wr_lo_ref[...]) + _dot(h_lo, wr_hi) + br_ref[...]
    h_ref[...] = h


def _row_copy(src_hbm, src_row, dst_vmem, dst_row, sem):
    return pltpu.make_async_copy(src_hbm.at[pl.ds(src_row, 1), :], dst_vmem.at[pl.ds(dst_row, 1), :], sem)


def _gather_rows_start(idx_ref, base, n_rows, src_hbm, dst_vmem, sem):
    def body(g, carry):
        for k in range(GATHER_UNROLL):
            r = g * GATHER_UNROLL + k
            _row_copy(src_hbm, idx_ref[base + r], dst_vmem, r, sem).start()
        return carry

    lax.fori_loop(0, n_rows // GATHER_UNROLL, body, 0)


def _gather_rows_wait(n_rows, src_hbm, dst_vmem, sem):
    def body(g, carry):
        for k in range(GATHER_UNROLL):
            _row_copy(src_hbm, 0, dst_vmem, 0, sem).wait()
        return carry

    lax.fori_loop(0, n_rows // GATHER_UNROLL, body, 0)


def _moe_expert_body(be_ref, nvalid_ref, tok_ref, h_hbm, wg_ref, wu_ref, bg_ref, bu_ref, wd_ref, bd_ref, gate_ref,
                     out_ref, xbuf, xbf, acc_ref, sem):
    blk, n = pl.program_id(0), pl.program_id(1)
    last = pl.num_programs(1) - 1
    n_valid = nvalid_ref[0]
    valid = blk < n_valid
    tm = xbf.shape[0]
    slot = blk % 2

    @pl.when(n == 0)
    def _():
        @pl.when(blk == 0)
        def _():
            _gather_rows_start(tok_ref, 0, tm, h_hbm, xbuf.at[0], sem.at[0])

        @pl.when(blk + 1 < n_valid)
        def _():
            _gather_rows_start(tok_ref, (blk + 1) * tm, tm, h_hbm, xbuf.at[1 - slot], sem.at[1 - slot])

        @pl.when(valid)
        def _():
            _gather_rows_wait(tm, h_hbm, xbuf.at[slot], sem.at[slot])
            xbf[...] = xbuf[slot].astype(BF16)
            acc_ref[...] = jnp.broadcast_to(bd_ref[...], acc_ref.shape)

    @pl.when(valid)
    def _():
        x = xbf[...]
        gl = jnp.minimum(_dot(x, wg_ref[...].astype(BF16)) + bg_ref[...], SWIGLU_LIMIT)
        up = jnp.clip(_dot(x, wu_ref[...].astype(BF16)) + bu_ref[...], -SWIGLU_LIMIT, SWIGLU_LIMIT)
        act = (up + 1.0) * gl * jax.nn.sigmoid(SWIGLU_ALPHA * gl)
        acc_ref[...] += _dot(act.astype(BF16), wd_ref[...].astype(BF16))

        @pl.when(n == last)
        def _():
            out_ref[...] = acc_ref[...] * gate_ref[...]

    @pl.when(jnp.logical_and(jnp.logical_not(valid), n == 0))
    def _():
        out_ref[...] = jnp.zeros_like(out_ref)


def _moe_layer(geom, layer, x, modx, norm_g, w_r, b_r, w_gu, b_gu, w_d, b_d):
    d, t = geom.d, geom.t
    n_exp = w_r.shape[-1]
    de = w_d.shape[2]
    lanes = 128
    e_pad = -(-n_exp // lanes) * lanes
    wr_hi, wr_lo = _split_bf16(jnp.pad(w_r, ((0, 0), (0, e_pad - n_exp))))
    br = jnp.pad(b_r, (0, e_pad - n_exp)).reshape(1, e_pad)

    h, logits = pl.pallas_call(
        _moe_prep_body,
        grid=(geom.nb,),
        in_specs=[geom.row(d), geom.full(norm_g), geom.mod(4), geom.mod(3),
                  geom.full(wr_hi), geom.full(wr_lo), geom.full(br)],
        out_specs=[geom.row(d), geom.row(e_pad)],
        out_shape=[jax.ShapeDtypeStruct((t, d), F32), jax.ShapeDtypeStruct((t, e_pad), F32)],
        compiler_params=_params(1),
        name="moe_prep",
    )(x, norm_g, modx, modx, wr_hi, wr_lo, br)

    tm = EXPERT_ROWS
    top_v, top_e = lax.top_k(logits[:, :n_exp], TOP_K)
    gates = jax.nn.softmax(top_v, axis=-1).reshape(-1)
    e_flat = top_e.reshape(-1).astype(jnp.int32)
    n_assign = t * TOP_K
    order = jnp.argsort(e_flat).astype(jnp.int32)
    rank = jnp.argsort(order).astype(jnp.int32)
    counts = jnp.sum((e_flat[:, None] == jnp.arange(n_exp, dtype=jnp.int32)[None, :]).astype(jnp.int32), axis=0)
    cum = jnp.cumsum(counts) - counts
    padded = (counts + tm - 1) // tm * tm
    pad_end = jnp.cumsum(padded)
    start = pad_end - padded
    n_blocks = -(-(n_assign + n_exp * (tm - 1)) // tm)
    n_slots = n_blocks * tm
    blk_e = jnp.minimum(jnp.searchsorted(pad_end, jnp.arange(n_blocks, dtype=jnp.int32) * tm, side='right'),
                        n_exp - 1).astype(jnp.int32)
    n_valid = (pad_end[-1] // tm).astype(jnp.int32).reshape(1)
    s_idx = jnp.arange(n_slots, dtype=jnp.int32)
    e_s = jnp.repeat(blk_e, tm)
    r_s = s_idx - start[e_s]
    live = (r_s < counts[e_s]) & (s_idx < pad_end[-1])
    a_s = order[jnp.clip(cum[e_s] + r_s, 0, n_assign - 1)]
    tok = jnp.where(live, a_s // TOP_K, 0).astype(jnp.int32)
    gate = jnp.where(live, gates[a_s], 0.0)
    slot_of = (start[e_flat] + rank - cum[e_flat]).astype(jnp.int32)

    tn = _pick(de, (512, 256, 128))
    nt = de // tn
    out = pl.pallas_call(
        _moe_expert_body,
        grid_spec=pltpu.PrefetchScalarGridSpec(
            num_scalar_prefetch=3,
            grid=(n_blocks, nt),
            in_specs=[pl.BlockSpec(memory_space=pl.ANY),
                      pl.BlockSpec((None, None, d, tn), lambda b, n, be, nv, tk: (layer, be[b], 0, n)),
                      pl.BlockSpec((None, None, d, tn), lambda b, n, be, nv, tk: (layer, be[b], 0, nt + n)),
                      pl.BlockSpec((None, None, 1, tn), lambda b, n, be, nv, tk: (layer, be[b], 0, n)),
                      pl.BlockSpec((None, None, 1, tn), lambda b, n, be, nv, tk: (layer, be[b], 0, nt + n)),
                      pl.BlockSpec((None, None, tn, d), lambda b, n, be, nv, tk: (layer, be[b], n, 0)),
                      pl.BlockSpec((None, None, 1, d), lambda b, n, be, nv, tk: (layer, be[b], 0, 0)),
                      pl.BlockSpec((tm, 1), lambda b, n, be, nv, tk: (b, 0))],
            out_specs=pl.BlockSpec((tm, d), lambda b, n, be, nv, tk: (b, 0)),
            scratch_shapes=[pltpu.VMEM((2, tm, d), F32), pltpu.VMEM((tm, d), BF16), pltpu.VMEM((tm, d), F32),
                            pltpu.SemaphoreType.DMA((2,))]),
        out_shape=jax.ShapeDtypeStruct((n_slots, d), F32),
        compiler_params=_params(2, gathers=True),
        name="moe_experts",
    )(blk_e, n_valid, tok, h, w_gu, w_gu, b_gu.reshape(*b_gu.shape[:2], 1, -1), b_gu.reshape(*b_gu.shape[:2], 1, -1),
      w_d, b_d.reshape(*b_d.shape[:2], 1, -1), gate.reshape(n_slots, 1))

    return _moe_combine(geom, x, out, slot_of, modx, 5)


def _moe_combine_body(slot_ref, out_hbm, x_ref, gate_ref, o_ref, ybuf, sem, *, top_k):
    i = pl.program_id(0)
    rb = x_ref.shape[0]
    n_rows = top_k * rb
    slot = i % 2

    @pl.when(i == 0)
    def _():
        _gather_rows_start(slot_ref, 0, n_rows, out_hbm, ybuf.at[0], sem.at[0])

    @pl.when(i + 1 < pl.num_programs(0))
    def _():
        _gather_rows_start(slot_ref, (i + 1) * n_rows, n_rows, out_hbm, ybuf.at[1 - slot], sem.at[1 - slot])

    _gather_rows_wait(n_rows, out_hbm, ybuf.at[slot], sem.at[slot])
    y = ybuf[slot, 0:rb, :]
    for k in range(1, top_k):
        y = y + ybuf[slot, k * rb:(k + 1) * rb, :]
    o_ref[...] = x_ref[...] + gate_ref[...] * y


def _moe_combine(geom, x, out, slot_of, modx, gate_chunk):
    d = geom.d
    return pl.pallas_call(
        functools.partial(_moe_combine_body, top_k=TOP_K),
        grid_spec=pltpu.PrefetchScalarGridSpec(
            num_scalar_prefetch=1,
            grid=(geom.nb,),
            in_specs=[pl.BlockSpec(memory_space=pl.ANY),
                      pl.BlockSpec((geom.rb, d), lambda i, sl: (i, 0)),
                      geom.mod(gate_chunk)],
            out_specs=pl.BlockSpec((geom.rb, d), lambda i, sl: (i, 0)),
            scratch_shapes=[pltpu.VMEM((2, TOP_K * geom.rb, d), F32), pltpu.SemaphoreType.DMA((2,))]),
        out_shape=jax.ShapeDtypeStruct((geom.t, d), F32),
        compiler_params=_params(1, gathers=True),
        name="moe_combine",
    )(slot_of.reshape(geom.nb, geom.rb, TOP_K).transpose(0, 2, 1).reshape(-1), out, x, modx)


def _rope_tables(pos, half):
    inv = ROPE_THETA ** (-jnp.arange(half, dtype=F32) / half)
    ang = pos.astype(F32)[:, None] * inv
    return jnp.cos(ang), jnp.sin(ang)


def _mla_down_body(x_ref, g_ref, sc_ref, sh_ref, w_ref, gqa_ref, gkva_ref,
                   cq_ref, ckv_ref, kr_ref, krs_ref, *, ql, kl, rope):
    h = _modulate(x_ref[...], g_ref[...], sc_ref[...], sh_ref[...]).astype(BF16)
    dd = _dot(h, w_ref[...])
    dq = dd[:, :ql]
    cq_ref[...] = (dq * lax.rsqrt(jnp.mean(dq * dq, axis=-1, keepdims=True) + NORM_EPS) * gqa_ref[...]).astype(BF16)
    dk = dd[:, ql:ql + kl]
    ckv_ref[...] = dk * lax.rsqrt(jnp.mean(dk * dk, axis=-1, keepdims=True) + NORM_EPS) * gkva_ref[...]
    kr_ref[...] = dd[:, ql + kl:ql + kl + rope]
    krs_ref[...] = dd[:, ql + kl + rope:]


def _mla_q_prompt_body(cq_ref, w_ref, a_ref, b_ref, q_ref, *, heads, hw, qk):
    cq = cq_ref[...]
    a, bt = a_ref[...], b_ref[...]
    for h in range(heads):
        x = _dot(cq, w_ref[:, h * hw:(h + 1) * hw])
        xs = _dot(cq, w_ref[:, (heads + h) * hw:(heads + h + 1) * hw])
        r = lax.rsqrt(jnp.sum(x * x, axis=-1, keepdims=True) * (1.0 / qk) + NORM_EPS)
        q_ref[:, h * hw:(h + 1) * hw] = ((x * a + xs * bt) * r).astype(BF16)


def _mla_q_sample_body(cq_ref, w_ref, a_ref, b_ref, a2_ref, b2_ref, wuk_ref, qabs_ref, qr_ref,
                       *, heads, hw, qk, nope, rope, nq_tok):
    cq = cq_ref[...]
    rb = cq.shape[0]
    nb = rb // nq_tok
    a, bt = a_ref[...], b_ref[...]
    a2, b2 = a2_ref[...], b2_ref[...]
    for h in range(heads):
        x = _dot(cq, w_ref[:, h * hw:(h + 1) * hw])
        xs = _dot(cq, w_ref[:, (heads + h) * hw:(heads + h + 1) * hw])
        r = lax.rsqrt(jnp.sum(x * x, axis=-1, keepdims=True) * (1.0 / qk) + NORM_EPS)
        qn = (x[:, :nope] * a[:, :nope] * r).astype(BF16)
        qabs = _dot_nt(qn, wuk_ref[:, h * nope:(h + 1) * nope])
        xr, xsr = x[:, nope:nope + rope], xs[:, nope:nope + rope]
        q1 = (xr * a[:, nope:nope + rope] + xsr * bt[:, nope:nope + rope]) * r
        q2 = (xsr * a2 + xr * b2) * r
        rows = slice(h * nq_tok, (h + 1) * nq_tok)
        qabs_ref[:, rows, :] = qabs.reshape(nb, nq_tok, qabs.shape[1])
        qr_ref[:, rows, :rope] = q1.reshape(nb, nq_tok, rope)
        qr_ref[:, rows, rope:] = q2.reshape(nb, nq_tok, rope)


def _mla_kv_prompt_body(ckv_ref, kr_ref, krs_ref, wuk_ref, wuv_ref, gkn_ref, ak_ref, bk_ref, k_ref, v_ref,
                        *, heads, hw, qk, nope, rope, vd):
    c = ckv_ref[...].astype(BF16)
    kr = kr_ref[...]
    ssr = jnp.sum(kr * kr, axis=-1, keepdims=True)
    krot = kr * ak_ref[...] + krs_ref[...] * bk_ref[...]
    gkn = gkn_ref[...]
    zeros = jnp.zeros((c.shape[0], hw - nope - rope), BF16)
    for h in range(heads):
        kn = _dot(c, wuk_ref[:, h * nope:(h + 1) * nope])
        r = lax.rsqrt((jnp.sum(kn * kn, axis=-1, keepdims=True) + ssr) * (1.0 / qk) + NORM_EPS)
        k_ref[:, h * hw:h * hw + nope] = (kn * gkn * r).astype(BF16)
        k_ref[:, h * hw + nope:h * hw + nope + rope] = (krot * r).astype(BF16)
        k_ref[:, h * hw + nope + rope:(h + 1) * hw] = zeros
        v_ref[:, h * vd:(h + 1) * vd] = _dot(c, wuv_ref[:, h * vd:(h + 1) * vd]).astype(BF16)


def _flash_body(qi_ref, kj_ref, q_ref, k_ref, v_ref, o_ref, m_ref, l_ref, acc_ref):
    n = pl.program_id(1)
    qi, kj = qi_ref[n], kj_ref[n]
    tq, tk = q_ref.shape[0], k_ref.shape[0]

    @pl.when(kj == 0)
    def _():
        m_ref[...] = jnp.full_like(m_ref, MASKED)
        l_ref[...] = jnp.zeros_like(l_ref)
        acc_ref[...] = jnp.zeros_like(acc_ref)

    s = _dot_nt(q_ref[...], k_ref[...])
    rows = qi * tq + lax.broadcasted_iota(jnp.int32, (tq, 1), 0)
    cols = kj * tk + lax.broadcasted_iota(jnp.int32, (1, tk), 1)
    s = jnp.where(cols <= rows, s, MASKED)
    m_old = m_ref[...]
    m_new = jnp.maximum(m_old, jnp.max(s, axis=-1, keepdims=True))
    p = jnp.exp(s - m_new)
    alpha = jnp.exp(m_old - m_new)
    l_ref[...] = l_ref[...] * alpha + jnp.sum(p, axis=-1, keepdims=True)
    acc_ref[...] = acc_ref[...] * alpha + _dot(p.astype(BF16), v_ref[...])
    m_ref[...] = m_new

    @pl.when(kj == qi)
    def _():
        o_ref[...] = (acc_ref[...] * (1.0 / l_ref[...])).astype(BF16)


def _mla_sample_body(pt_ref, qabs_ref, qr_ref, cnew_ref, krnew_ref, cs_ref, csn_ref, wukt_ref, ckv_hbm, krt_hbm,
                     out_ref, cbuf, krbuf, sem_c, sem_k, qabs_bf, qr_bf, m_ref, l_ref, acc_ref,
                     *, layer, heads, nq_tok, nope, qk, pages, chunk_pages):
    b, j = pl.program_id(0), pl.program_id(1)
    nj = pl.num_programs(1)
    step = b * nj + j
    last_step = pl.num_programs(0) * nj - 1
    slot = step % 2
    hq = heads * nq_tok
    page = cbuf.shape[2]

    def page_copies(step_, slot_, u):
        phys = pt_ref[step_ * pages + u]
        return (pltpu.make_async_copy(ckv_hbm.at[layer, phys], cbuf.at[slot_, u], sem_c.at[slot_]),
                pltpu.make_async_copy(krt_hbm.at[layer, phys], krbuf.at[slot_, u], sem_k.at[slot_]))

    def start_pages(step_, slot_, us):
        for u in us:
            for cp in page_copies(step_, slot_, u):
                cp.start()

    def wait_pages(step_, slot_):
        for u in range(pages):
            for cp in page_copies(step_, slot_, u):
                cp.wait()

    @pl.when(step == 0)
    def _():
        start_pages(0, 0, range(pages))

    wait_pages(step, slot)
    next_step = jnp.minimum(step + 1, last_step)

    @pl.when(j == 0)
    def _():
        m_ref[...] = jnp.full_like(m_ref, MASKED)
        l_ref[...] = jnp.zeros_like(l_ref)
        acc_ref[...] = jnp.zeros_like(acc_ref)
        qabs_bf[...] = qabs_ref[...].astype(BF16)
        qr_bf[...] = qr_ref[...].astype(BF16)

    def scores(c_bf, kr_t, cs):
        n = c_bf.shape[0]
        ssn_parts = []
        for h0 in range(0, heads, SAMPLE_HEAD_GROUP):
            kt = _dot_nt(wukt_ref[h0 * nope:(h0 + SAMPLE_HEAD_GROUP) * nope, :], c_bf)
            ssn_parts.append(jnp.sum((kt * kt).reshape(SAMPLE_HEAD_GROUP, nope, n), axis=1))
        ssn = jnp.concatenate(ssn_parts, axis=0)
        ssr = jnp.sum(kr_t * kr_t, axis=0, keepdims=True)
        rk = lax.rsqrt((ssn + ssr) * (1.0 / qk) + NORM_EPS)
        k_rot = (jnp.concatenate([kr_t, kr_t], axis=0) * cs).astype(BF16)
        s = _dot_nt(qabs_bf[...], c_bf) + _dot(qr_bf[...], k_rot)
        return s * jnp.broadcast_to(rk[:, None, :], (heads, nq_tok, n)).reshape(hq, n)

    def update(s, c_bf):
        m_old = m_ref[...]
        m_new = jnp.maximum(m_old, jnp.max(s, axis=-1, keepdims=True))
        p = jnp.exp(s - m_new)
        alpha = jnp.exp(m_old - m_new)
        l_ref[...] = l_ref[...] * alpha + jnp.sum(p, axis=-1, keepdims=True)
        acc_ref[...] = acc_ref[...] * alpha + _dot(p.astype(BF16), c_bf)
        m_ref[...] = m_new

    s_parts, c_parts = [], []
    for u in range(0, pages, chunk_pages):
        us = range(u, u + chunk_pages)
        c_bf = jnp.concatenate([cbuf[slot, v].astype(BF16) for v in us], axis=0)
        kr_t = jnp.concatenate([krbuf[slot, v] for v in us], axis=1)
        s_parts.append(scores(c_bf, kr_t, cs_ref[:, u * page:(u + chunk_pages) * page]))
        c_parts.append(c_bf)
        start_pages(next_step, 1 - slot, us)
    update(jnp.concatenate(s_parts, axis=1), jnp.concatenate(c_parts, axis=0))

    @pl.when(j == nj - 1)
    def _():
        n_new = cnew_ref.shape[0]
        row_q = lax.broadcasted_iota(jnp.int32, (hq, n_new), 0) % nq_tok
        col = lax.broadcasted_iota(jnp.int32, (hq, n_new), 1)
        c_bf = cnew_ref[...].astype(BF16)
        s = scores(c_bf, krnew_ref[...], csn_ref[...])
        update(jnp.where(col <= row_q, s, MASKED), c_bf)
        out_ref[...] = acc_ref[...] * (1.0 / l_ref[...])

    @pl.when(step == last_step)
    def _():
        wait_pages(last_step, 1 - slot)


def _mla_sample_attn(layer, page_table, qabs, qr, cnew, krnew_t, cache_ckv, cache_kr_t, cs_old, cs_new, w_uk_t,
                     *, heads, nq_tok, nope, qk):
    db, hq, kl = qabs.shape
    n_pages = page_table.shape[1]
    page = cache_ckv.shape[2]
    rope = cache_kr_t.shape[2]
    n_new = cnew.shape[1]
    pg = _pick(n_pages, (PAGES_PER_STEP, 8, 4, 2, 1))
    chunk_pages = _pick(pg, (SAMPLE_CHUNK_PAGES, 2, 1))
    return pl.pallas_call(
        functools.partial(_mla_sample_body, layer=layer, heads=heads, nq_tok=nq_tok, nope=nope, qk=qk, pages=pg,
                          chunk_pages=chunk_pages),
        grid_spec=pltpu.PrefetchScalarGridSpec(
            num_scalar_prefetch=1,
            grid=(db, n_pages // pg),
            in_specs=[pl.BlockSpec((None, hq, kl), lambda b, j, pt: (b, 0, 0)),
                      pl.BlockSpec((None, hq, 2 * rope), lambda b, j, pt: (b, 0, 0)),
                      pl.BlockSpec((None, n_new, kl), lambda b, j, pt: (b, 0, 0)),
                      pl.BlockSpec((None, rope, n_new), lambda b, j, pt: (b, 0, 0)),
                      pl.BlockSpec((2 * rope, pg * page), lambda b, j, pt: (0, j)),
                      pl.BlockSpec((2 * rope, n_new), lambda b, j, pt: (0, 0)),
                      pl.BlockSpec(w_uk_t.shape, lambda b, j, pt: (0, 0)),
                      pl.BlockSpec(memory_space=pl.ANY),
                      pl.BlockSpec(memory_space=pl.ANY)],
            out_specs=pl.BlockSpec((None, hq, kl), lambda b, j, pt: (b, 0, 0)),
            scratch_shapes=[pltpu.VMEM((2, pg, page, kl), F32), pltpu.VMEM((2, pg, rope, page), F32),
                            pltpu.SemaphoreType.DMA((2,)), pltpu.SemaphoreType.DMA((2,)),
                            pltpu.VMEM((hq, kl), BF16), pltpu.VMEM((hq, 2 * rope), BF16),
                            pltpu.VMEM((hq, 1), F32), pltpu.VMEM((hq, 1), F32), pltpu.VMEM((hq, kl), F32)]),
        out_shape=jax.ShapeDtypeStruct((db, hq, kl), F32),
        compiler_params=_params(2, gathers=True),
        name="mla_sample_attn",
    )(page_table.reshape(-1), qabs, qr, cnew, krnew_t, cs_old, cs_new, w_uk_t, cache_ckv, cache_kr_t)


def _mla_uv_body(lat_ref, wuv_ref, o_ref, *, heads, nq_tok, vd):
    nb, _, kl = lat_ref.shape
    for h in range(heads):
        lat = lat_ref[:, h * nq_tok:(h + 1) * nq_tok, :].reshape(nb * nq_tok, kl).astype(BF16)
        o_ref[:, h * vd:(h + 1) * vd] = _dot(lat, wuv_ref[:, h * vd:(h + 1) * vd])


def _mla_layer(geom, layer, x, modx, norm_g, w_dkv, g_qa, g_kva, w_uq, g_q, w_uk, g_k, w_uv, w_o,
               cache_ckv, cache_kr, page_table, past_len):
    d, rb, qt = geom.d, geom.rb, geom.q
    ql, kl = g_qa.shape[0], g_kva.shape[0]
    rope = cache_kr.shape[-1]
    half = rope // 2
    qk = g_q.shape[0]
    nope = qk - rope
    heads = w_uq.shape[1] // qk
    vd = w_uv.shape[1] // heads
    hw = MXU_WIDTH
    scale = qk ** -0.5
    page = cache_ckv.shape[2]
    n_pages = page_table.shape[1]
    tp, ts = geom.tp, geom.ts

    w_dn = jnp.concatenate([w_dkv, w_dkv[:, ql + kl + half:], w_dkv[:, ql + kl:ql + kl + half]], axis=1).astype(BF16)
    gqa, gkva = g_qa.reshape(1, ql), g_kva.reshape(1, kl)
    cq, ckv, kr, krs = pl.pallas_call(
        functools.partial(_mla_down_body, ql=ql, kl=kl, rope=rope),
        grid=(geom.nb,),
        in_specs=[geom.row(d), geom.full(norm_g), geom.mod(1), geom.mod(0), geom.full(w_dn),
                  geom.full(gqa), geom.full(gkva)],
        out_specs=[geom.row(ql), geom.row(kl), geom.row(rope), geom.row(rope)],
        out_shape=[jax.ShapeDtypeStruct((geom.t, ql), BF16), jax.ShapeDtypeStruct((geom.t, kl), F32),
                   jax.ShapeDtypeStruct((geom.t, rope), F32), jax.ShapeDtypeStruct((geom.t, rope), F32)],
        compiler_params=_params(1),
        name="mla_down",
    )(x, norm_g, modx, modx, w_dn, gqa, gkva)

    w3 = w_uq.reshape(ql, heads, qk)
    wn, w1, w2 = w3[:, :, :nope], w3[:, :, nope:nope + half], w3[:, :, nope + half:]
    zpad = jnp.zeros((ql, heads, hw - qk), F32)
    w_q = jnp.concatenate([jnp.concatenate([wn, w1, w2, zpad], axis=-1).reshape(ql, heads * hw),
                           jnp.concatenate([jnp.zeros_like(wn), w2, w1, zpad], axis=-1).reshape(ql, heads * hw)],
                          axis=1).astype(BF16)
    gqn, gq1, gq2 = g_q[:nope], g_q[nope:nope + half], g_q[nope + half:]
    gkn, gk1, gk2 = g_k[:nope], g_k[nope:nope + half], g_k[nope + half:]

    def q_tables(pos, nope_gain):
        cos, sin = _rope_tables(pos, half)
        n = pos.shape[0]
        z = jnp.zeros((n, hw - qk), F32)
        a = jnp.concatenate([jnp.broadcast_to(nope_gain, (n, nope)), gq1 * cos, gq2 * cos, z], axis=1) * scale
        b = jnp.concatenate([jnp.zeros((n, nope), F32), -gq2 * sin, gq1 * sin, z], axis=1) * scale
        return a, b, cos, sin

    pos_p = jnp.tile(jnp.arange(geom.s, dtype=jnp.int32), geom.b)
    pos_s = jnp.tile(past_len + jnp.arange(qt, dtype=jnp.int32), geom.db)
    a_p, b_p, cos_p, sin_p = q_tables(pos_p, gqn)
    a_s, b_s, cos_s, sin_s = q_tables(pos_s, gqn * gkn)
    a2_s = jnp.concatenate([gq2 * cos_s, -gq1 * cos_s], axis=1) * scale
    b2_s = jnp.concatenate([gq1 * sin_s, gq2 * sin_s], axis=1) * scale

    q_p = pl.pallas_call(
        functools.partial(_mla_q_prompt_body, heads=heads, hw=hw, qk=qk),
        grid=(geom.nbp,),
        in_specs=[geom.row(ql), geom.full(w_q), geom.row(hw), geom.row(hw)],
        out_specs=geom.row(heads * hw),
        out_shape=jax.ShapeDtypeStruct((tp, heads * hw), BF16),
        compiler_params=_params(1),
        name="mla_q_prompt",
    )(cq, w_q, a_p, b_p)

    w_uk_bf = w_uk.astype(BF16)
    nbq = rb // qt
    hq = heads * qt
    qabs, qr = pl.pallas_call(
        functools.partial(_mla_q_sample_body, heads=heads, hw=hw, qk=qk, nope=nope, rope=rope, nq_tok=qt),
        grid=(geom.nbs,),
        in_specs=[geom.row(ql, geom.nbp), geom.full(w_q), geom.row(hw), geom.row(hw), geom.row(rope), geom.row(rope),
                  geom.full(w_uk_bf)],
        out_specs=[pl.BlockSpec((nbq, hq, kl), lambda i: (i, 0, 0)),
                   pl.BlockSpec((nbq, hq, 2 * rope), lambda i: (i, 0, 0))],
        out_shape=[jax.ShapeDtypeStruct((geom.db, hq, kl), F32), jax.ShapeDtypeStruct((geom.db, hq, 2 * rope), F32)],
        compiler_params=_params(1),
        name="mla_q_sample",
    )(cq, w_q, a_s, b_s, a2_s, b2_s, w_uk_bf)

    w_uv_bf = w_uv.astype(BF16)
    a_k = jnp.concatenate([gk1 * cos_p, gk2 * cos_p], axis=1)
    b_k = jnp.concatenate([-gk2 * sin_p, gk1 * sin_p], axis=1)
    gkn_row = gkn.reshape(1, nope)
    k_p, v_p = pl.pallas_call(
        functools.partial(_mla_kv_prompt_body, heads=heads, hw=hw, qk=qk, nope=nope, rope=rope, vd=vd),
        grid=(geom.nbp,),
        in_specs=[geom.row(kl), geom.row(rope), geom.row(rope), geom.full(w_uk_bf), geom.full(w_uv_bf),
                  geom.full(gkn_row), geom.row(rope), geom.row(rope)],
        out_specs=[geom.row(heads * hw), geom.row(heads * vd)],
        out_shape=[jax.ShapeDtypeStruct((tp, heads * hw), BF16), jax.ShapeDtypeStruct((tp, heads * vd), BF16)],
        compiler_params=_params(1),
        name="mla_kv_prompt",
    )(ckv, kr, krs, w_uk_bf, w_uv_bf, gkn_row, a_k, b_k)

    o_parts = []
    tq = _pick(geom.s, (1024, 512, 256, 128, 64))
    nqb = geom.s // tq
    pairs = [(i, j) for i in range(nqb) for j in range(i + 1)]
    qi_tab = jnp.asarray([p[0] for p in pairs], jnp.int32)
    kj_tab = jnp.asarray([p[1] for p in pairs], jnp.int32)
    for bi in range(geom.b):
        off = bi * nqb
        o_parts.append(pl.pallas_call(
            _flash_body,
            grid_spec=pltpu.PrefetchScalarGridSpec(
                num_scalar_prefetch=2,
                grid=(heads, len(pairs)),
                in_specs=[pl.BlockSpec((tq, hw), lambda h, n, qi, kj: (off + qi[n], h)),
                          pl.BlockSpec((tq, hw), lambda h, n, qi, kj: (off + kj[n], h)),
                          pl.BlockSpec((tq, vd), lambda h, n, qi, kj: (off + kj[n], h))],
                out_specs=pl.BlockSpec((tq, vd), lambda h, n, qi, kj: (qi[n], h)),
                scratch_shapes=[pltpu.VMEM((tq, 1), F32), pltpu.VMEM((tq, 1), F32), pltpu.VMEM((tq, vd), F32)]),
            out_shape=jax.ShapeDtypeStruct((geom.s, heads * vd), BF16),
            compiler_params=_params(2),
            name="mla_prompt_attn",
        )(qi_tab, kj_tab, q_p, k_p, v_p))

    n_new = 128
    ckv_s = ckv[tp:].reshape(geom.db, qt, kl)
    kr_s = kr[tp:].reshape(geom.db, qt, rope)
    cnew = jnp.pad(ckv_s, ((0, 0), (0, n_new - qt), (0, 0)))
    krnew_t = jnp.pad(kr_s, ((0, 0), (0, n_new - qt), (0, 0))).transpose(0, 2, 1)
    cos_c, sin_c = _rope_tables(jnp.arange(past_len + n_new, dtype=jnp.int32), half)
    cs = jnp.concatenate([gk1 * cos_c, gk2 * cos_c, gk1 * sin_c, gk2 * sin_c], axis=1).T
    o_lat = _mla_sample_attn(layer, page_table, qabs, qr, cnew, krnew_t, cache_ckv, cache_kr.transpose(0, 1, 3, 2),
                             cs[:, :past_len], cs[:, past_len:], w_uk.T.astype(BF16),
                             heads=heads, nq_tok=qt, nope=nope, qk=qk)

    o_s = pl.pallas_call(
        functools.partial(_mla_uv_body, heads=heads, nq_tok=qt, vd=vd),
        grid=(geom.nbs,),
        in_specs=[pl.BlockSpec((nbq, hq, kl), lambda i: (i, 0, 0)), geom.full(w_uv_bf)],
        out_specs=geom.row(heads * vd),
        out_shape=jax.ShapeDtypeStruct((ts, heads * vd), F32),
        compiler_params=_params(1),
        name="mla_uv",
    )(o_lat, w_uv_bf)

    o_all = jnp.concatenate(o_parts + [o_s.astype(BF16)], axis=0)
    x = _out_proj(geom, o_all, w_o, None, x, modx, 2)
    return (x, ckv[:tp].reshape(geom.b, geom.s, kl), kr[:tp].reshape(geom.b, geom.s, rope), ckv_s, kr_s)


def kernel(x_prompt, x_sample, state_swa_k, state_swa_v, cache_mla_ckv, cache_mla_kr, page_table, c_prompt, c_sample, ada_w, ada_b, norm_attn_g, norm_ffn_g, swa_w_qkv, swa_b_qkv, swa_g_q, swa_g_k, swa_sinks, swa_w_o, swa_b_o, mla_w_dkv, mla_g_qa, mla_g_kva, mla_w_uq, mla_g_q, mla_w_uk, mla_g_k, mla_w_uv, mla_w_o, moe_w_router, moe_b_router, moe_w_gate_up, moe_b_gate_up, moe_w_down, moe_b_down):
    b, s, d = x_prompt.shape
    db, qt, _ = x_sample.shape
    depth = ada_w.shape[0]
    past_len = page_table.shape[1] * cache_mla_ckv.shape[2]
    geom = _Geom(b, s, db, qt, d)

    n_c = b + db
    m_pad = -(-n_c // 8) * 8
    c_all = jnp.pad(jnp.concatenate([c_prompt, c_sample], axis=0), ((0, m_pad - n_c), (0, 0)))
    mod = _adaln(c_all, ada_w, ada_b)

    x = jnp.concatenate([x_prompt.reshape(b * s, d), x_sample.reshape(db * qt, d)], axis=0)
    swa_kp, swa_vp, swa_ks, swa_vs = [], [], [], []
    ckv_p, kr_p, ckv_s, kr_s = [], [], [], []
    n_mixers = 2
    for i in range(depth):
        modx = geom.expand_mod(mod[i])
        g_attn = norm_attn_g[i].reshape(1, d)
        g_ffn = norm_ffn_g[i].reshape(1, d)
        j = i // n_mixers
        if i % n_mixers == 0:
            x, kp, vp, ks, vs = _swa_layer(geom, x, modx, g_attn, swa_w_qkv[j], swa_b_qkv[j], swa_g_q[j], swa_g_k[j],
                                           swa_sinks[j], swa_w_o[j], swa_b_o[j], state_swa_k[j], state_swa_v[j], past_len)
            swa_kp.append(kp)
            swa_vp.append(vp)
            swa_ks.append(ks)
            swa_vs.append(vs)
        else:
            x, cp, rp, cs, rs = _mla_layer(geom, j, x, modx, g_attn, mla_w_dkv[j], mla_g_qa[j], mla_g_kva[j], mla_w_uq[j],
                                           mla_g_q[j], mla_w_uk[j], mla_g_k[j], mla_w_uv[j], mla_w_o[j],
                                           cache_mla_ckv, cache_mla_kr, page_table, past_len)
            ckv_p.append(cp)
            kr_p.append(rp)
            ckv_s.append(cs)
            kr_s.append(rs)
        x = _moe_layer(geom, i, x, modx, g_ffn, moe_w_router[i], moe_b_router[i], moe_w_gate_up, moe_b_gate_up,
                       moe_w_down, moe_b_down)

    y_prompt = x[:b * s].reshape(b, s, d)
    y_sample = x[b * s:].reshape(db, qt, d)
    return (y_prompt, y_sample, jnp.stack(swa_kp), jnp.stack(swa_vp), jnp.stack(swa_ks), jnp.stack(swa_vs),
            jnp.stack(ckv_p), jnp.stack(kr_p), jnp.stack(ckv_s), jnp.stack(kr_s))
```

```python
import functools

import jax
import jax.numpy as jnp
from jax import lax
from jax.experimental import pallas as pl
from jax.experimental.pallas import tpu as pltpu

F32 = jnp.float32
BF16 = jnp.bfloat16

WINDOW = 128
ALIBI_MAX_BIAS = 8.0
ROPE_THETA = 10000.0
TOP_K = 4
SWIGLU_LIMIT = 7.0
SWIGLU_ALPHA = 1.702
NORM_EPS = 1e-6

MASKED = -1e30
VMEM_LIMIT_BYTES = 56 * 1024 * 1024
MXU_WIDTH = 256
EXPERT_ROWS = 1024
EXPERT_COLS = 256
GATHER_UNROLL = 8
PAGES_PER_STEP = 32
FLASH_ROWS = 2048
FLASH_SUB_ROWS = 256
SAMPLE_CHUNK_PAGES = 2
SAMPLE_HEAD_GROUP = 16


def _pick(n, prefs):
    for p in prefs:
        if n % p == 0:
            return p
    raise ValueError(f"no tile in {prefs} divides {n}")


def _dot(a, b):
    return jnp.dot(a, b, preferred_element_type=F32)


def _dot_nt(a, b):
    return lax.dot_general(a, b, (((1,), (1,)), ((), ())), preferred_element_type=F32)


def _split_bf16(x):
    hi = x.astype(BF16)
    lo = (x - hi.astype(F32)).astype(BF16)
    return hi, lo


def _params(n_axes, gathers=False):
    return pltpu.CompilerParams(dimension_semantics=("arbitrary",) * n_axes,
                                vmem_limit_bytes=VMEM_LIMIT_BYTES, disable_bounds_checks=gathers)


def _modulate(x, g, scale, shift):
    y = x * lax.rsqrt(jnp.mean(x * x, axis=-1, keepdims=True) + NORM_EPS) * g
    return y * (1.0 + scale) + shift


def _adaln_body(c_ref, w_ref, b_ref, o_ref):
    c = c_ref[...]
    a = (c * jax.nn.sigmoid(c)).astype(BF16)
    o_ref[...] = _dot(a, w_ref[...].astype(BF16)) + b_ref[...]


def _adaln(c_all, ada_w, ada_b):
    depth, d, n = ada_w.shape
    m = c_all.shape[0]
    tn = _pick(n, (1024, 512, 256, 128))
    return pl.pallas_call(
        _adaln_body,
        grid=(depth, n // tn),
        in_specs=[pl.BlockSpec((m, d), lambda l, j: (0, 0)),
                  pl.BlockSpec((None, d, tn), lambda l, j: (l, 0, j)),
                  pl.BlockSpec((None, 1, tn), lambda l, j: (l, 0, j))],
        out_specs=pl.BlockSpec((None, m, tn), lambda l, j: (l, 0, j)),
        out_shape=jax.ShapeDtypeStruct((depth, m, n), F32),
        compiler_params=_params(2),
        name="adaln",
    )(c_all, ada_w, ada_b.reshape(depth, 1, n))


class _Geom:
    def __init__(self, b, s, db, q, d):
        self.b, self.s, self.db, self.q, self.d = b, s, db, q, d
        self.tp, self.ts = b * s, db * q
        self.t = self.tp + self.ts
        rb = 256
        while s % rb or self.ts % rb or rb % q:
            rb //= 2
            if rb < 8:
                raise ValueError("token counts must be multiples of 8")
        self.rb = rb
        self.nbp, self.nbs = self.tp // rb, self.ts // rb
        self.nb = self.nbp + self.nbs

    def row(self, width, offset=0):
        return pl.BlockSpec((self.rb, width), lambda i: (i + offset, 0))

    def full(self, arr):
        nd = arr.ndim
        return pl.BlockSpec(arr.shape, lambda i: (0,) * nd)

    def mod(self, chunk, offset=0):
        nbp, spb, b = self.nbp, self.s // self.rb, self.b

        def index(i, *_):
            i = i + offset
            return (jnp.where(i < nbp, i // spb, b + i - nbp), chunk)

        return pl.BlockSpec((self.rb, self.d), index)

    def expand_mod(self, mod):
        return jnp.concatenate([jnp.repeat(mod[:self.b], self.rb, axis=0),
                                jnp.repeat(mod[self.b:self.b + self.db], self.q, axis=0)], axis=0)


def _swa_proj_body(x_ref, g_ref, sc_ref, sh_ref, w_ref, b_ref, gq_ref, gk_ref, e_ref,
                   q_ref, k_ref, v_ref, *, nq, nk, hd):
    h = _modulate(x_ref[...], g_ref[...], sc_ref[...], sh_ref[...]).astype(BF16)
    qkv = _dot(h, w_ref[...]) + b_ref[...]
    e = e_ref[...]
    inv_hd = 1.0 / hd
    for c in range(nq // nk):
        qc = qkv[:, c * nk:(c + 1) * nk]
        ss = _dot((qc * qc).astype(BF16), e)
        q_ref[:, c * nk:(c + 1) * nk] = (qc * lax.rsqrt(ss * inv_hd + NORM_EPS) * gq_ref[...]).astype(BF16)
    kc = qkv[:, nq:nq + nk]
    ss = _dot((kc * kc).astype(BF16), e)
    k_ref[...] = kc * lax.rsqrt(ss * inv_hd + NORM_EPS) * gk_ref[...]
    v_ref[...] = qkv[:, nq + nk:]


def _swa_prompt_body(sink_ref, q_ref, kc_ref, kp_ref, vc_ref, vp_ref, bias_ref, o_ref, *, kv, grp, hd):
    i = pl.program_id(0)
    w = q_ref.shape[0]
    col = lax.broadcasted_iota(jnp.int32, (1, 2 * w), 1)
    no_prev = jnp.where(jnp.logical_and(i == 0, col < w), MASKED, 0.0)
    for g in range(kv):
        ks = slice(g * hd, (g + 1) * hd)
        kcat = jnp.concatenate([kp_ref[:, ks], kc_ref[:, ks]], axis=0).astype(BF16)
        vcat = jnp.concatenate([vp_ref[:, ks], vc_ref[:, ks]], axis=0).astype(BF16)
        qg = jnp.concatenate([q_ref[:, (g * grp + j) * hd:(g * grp + j + 1) * hd] for j in range(grp)], axis=0)
        s = _dot_nt(qg, kcat) + bias_ref[g] + no_prev
        sink = jnp.concatenate([jnp.full((w, 1), sink_ref[g * grp + j], F32) for j in range(grp)], axis=0)
        m = jnp.maximum(jnp.max(s, axis=-1, keepdims=True), sink)
        p = jnp.exp(s - m)
        den = jnp.sum(p, axis=-1, keepdims=True) + jnp.exp(sink - m)
        og = _dot((p * (1.0 / den)).astype(BF16), vcat)
        for j in range(grp):
            o_ref[:, (g * grp + j) * hd:(g * grp + j + 1) * hd] = og[j * w:(j + 1) * w, :].astype(BF16)


def _swa_sample_body(sink_ref, q_ref, kn_ref, vn_ref, ks_ref, vs_ref, bias_old_ref, bias_new_ref,
                     o_ref, ko_ref, vo_ref, *, kv, grp, hd, nq_tok):
    nb, nbuf, _ = ks_ref.shape
    qt = nq_tok
    qf = q_ref[...].astype(F32)
    pad = jnp.zeros((nbuf - qt, hd), F32)
    for b in range(nb):
        rows = slice(b * qt, (b + 1) * qt)
        knew, vnew = kn_ref[rows, :], vn_ref[rows, :]
        ko_ref[b, :nbuf - qt, :] = ks_ref[b, qt:, :]
        ko_ref[b, nbuf - qt:, :] = knew
        vo_ref[b, :nbuf - qt, :] = vs_ref[b, qt:, :]
        vo_ref[b, nbuf - qt:, :] = vnew
        for g in range(kv):
            cs = slice(g * hd, (g + 1) * hd)
            k_old = ks_ref[b, :, cs].astype(BF16)
            v_old = vs_ref[b, :, cs].astype(BF16)
            k_new = jnp.concatenate([knew[:, cs], pad], axis=0).astype(BF16)
            v_new = jnp.concatenate([vnew[:, cs], pad], axis=0).astype(BF16)
            qg = jnp.concatenate([qf[rows, (g * grp + j) * hd:(g * grp + j + 1) * hd] for j in range(grp)],
                                 axis=0).astype(BF16)
            s_old = _dot_nt(qg, k_old) + bias_old_ref[g]
            s_new = _dot_nt(qg, k_new) + bias_new_ref[g]
            sink = jnp.concatenate([jnp.full((qt, 1), sink_ref[g * grp + j], F32) for j in range(grp)], axis=0)
            m = jnp.maximum(jnp.maximum(jnp.max(s_old, axis=-1, keepdims=True),
                                        jnp.max(s_new, axis=-1, keepdims=True)), sink)
            p_old = jnp.exp(s_old - m)
            p_new = jnp.exp(s_new - m)
            den = (jnp.sum(p_old, axis=-1, keepdims=True) + jnp.sum(p_new, axis=-1, keepdims=True)
                   + jnp.exp(sink - m))
            inv = 1.0 / den
            og = _dot((p_old * inv).astype(BF16), v_old) + _dot((p_new * inv).astype(BF16), v_new)
            for j in range(grp):
                o_ref[rows, (g * grp + j) * hd:(g * grp + j + 1) * hd] = og[j * qt:(j + 1) * qt, :]


def _alibi_bias(dist, valid, heads, kv):
    slopes = jnp.exp2(-ALIBI_MAX_BIAS * jnp.arange(1, heads + 1, dtype=F32) / heads)
    bias = jnp.where(valid[None], -slopes[:, None, None] * dist[None].astype(F32), MASKED)
    return bias.reshape(kv, (heads // kv) * dist.shape[0], dist.shape[1])


def _swa_layer(geom, x, modx, norm_g, w_qkv, b_qkv, g_q, g_k, sinks, w_o, b_o, state_k, state_v, past_len):
    d, rb = geom.d, geom.rb
    heads = sinks.shape[0]
    hd = g_q.shape[0]
    kv = state_k.shape[-2]
    grp = heads // kv
    nq, nk = heads * hd, kv * hd
    nbuf = state_k.shape[1]
    qt = geom.q
    if WINDOW != nbuf or geom.s % WINDOW:
        raise ValueError("window buffer must hold exactly one window and tile the prompt")

    head_of = jnp.arange(nk) // hd
    e = (head_of[:, None] == head_of[None, :]).astype(BF16)
    gq_t = (jnp.tile(g_q, kv) * hd ** -0.5).reshape(1, nk)
    gk_t = jnp.tile(g_k, kv).reshape(1, nk)

    q, k, v = pl.pallas_call(
        functools.partial(_swa_proj_body, nq=nq, nk=nk, hd=hd),
        grid=(geom.nb,),
        in_specs=[geom.row(d), geom.full(norm_g), geom.mod(1), geom.mod(0),
                  pl.BlockSpec(w_qkv.shape, lambda i: (0, 0)), pl.BlockSpec((1, nq + 2 * nk), lambda i: (0, 0)),
                  geom.full(gq_t), geom.full(gk_t), geom.full(e)],
        out_specs=[geom.row(nq), geom.row(nk), geom.row(nk)],
        out_shape=[jax.ShapeDtypeStruct((geom.t, nq), BF16), jax.ShapeDtypeStruct((geom.t, nk), F32),
                   jax.ShapeDtypeStruct((geom.t, nk), F32)],
        compiler_params=_params(1),
        name="swa_proj",
    )(x, norm_g, modx, modx, w_qkv.astype(BF16), b_qkv.reshape(1, -1), gq_t, gk_t, e)

    w = WINDOW
    t_idx = jnp.arange(w)[:, None]
    s_idx = jnp.arange(2 * w)[None, :]
    dist = t_idx + w - s_idx
    bias_p = _alibi_bias(dist, (dist >= 0) & (dist < WINDOW), heads, kv)
    nblk = geom.tp // w
    bpb = geom.s // w

    def prev_block(i):
        return (jnp.where(i % bpb == 0, i, i - 1), 0)

    o_p = pl.pallas_call(
        functools.partial(_swa_prompt_body, kv=kv, grp=grp, hd=hd),
        grid=(nblk,),
        in_specs=[pl.BlockSpec(memory_space=pltpu.SMEM),
                  pl.BlockSpec((w, nq), lambda i: (i, 0)),
                  pl.BlockSpec((w, nk), lambda i: (i, 0)), pl.BlockSpec((w, nk), prev_block),
                  pl.BlockSpec((w, nk), lambda i: (i, 0)), pl.BlockSpec((w, nk), prev_block),
                  pl.BlockSpec(bias_p.shape, lambda i: (0, 0, 0))],
        out_specs=pl.BlockSpec((w, nq), lambda i: (i, 0)),
        out_shape=jax.ShapeDtypeStruct((geom.tp, nq), BF16),
        compiler_params=_params(1),
        name="swa_prompt_attn",
    )(sinks, q, k, k, v, v, bias_p)

    qi = jnp.arange(qt)[:, None]
    so = jnp.arange(nbuf)[None, :]
    dist_old = qi + nbuf - so
    bias_old = _alibi_bias(dist_old, (dist_old < WINDOW) & (past_len - nbuf + so >= 0), heads, kv)
    dist_new = qi - so
    bias_new = _alibi_bias(dist_new, (dist_new >= 0) & (so < qt), heads, kv)
    nbt = _pick(geom.db, (8, 4, 2, 1))
    rows = nbt * qt
    off = geom.tp // rows
    o_s, k_s, v_s = pl.pallas_call(
        functools.partial(_swa_sample_body, kv=kv, grp=grp, hd=hd, nq_tok=qt),
        grid=(geom.db // nbt,),
        in_specs=[pl.BlockSpec(memory_space=pltpu.SMEM),
                  pl.BlockSpec((rows, nq), lambda i: (i + off, 0)),
                  pl.BlockSpec((rows, nk), lambda i: (i + off, 0)), pl.BlockSpec((rows, nk), lambda i: (i + off, 0)),
                  pl.BlockSpec((nbt, nbuf, nk), lambda i: (i, 0, 0)), pl.BlockSpec((nbt, nbuf, nk), lambda i: (i, 0, 0)),
                  pl.BlockSpec(bias_old.shape, lambda i: (0, 0, 0)), pl.BlockSpec(bias_new.shape, lambda i: (0, 0, 0))],
        out_specs=[pl.BlockSpec((rows, nq), lambda i: (i, 0)),
                   pl.BlockSpec((nbt, nbuf, nk), lambda i: (i, 0, 0)), pl.BlockSpec((nbt, nbuf, nk), lambda i: (i, 0, 0))],
        out_shape=[jax.ShapeDtypeStruct((geom.ts, nq), F32),
                   jax.ShapeDtypeStruct((geom.db, nbuf, nk), F32), jax.ShapeDtypeStruct((geom.db, nbuf, nk), F32)],
        compiler_params=_params(1),
        name="swa_sample_attn",
    )(sinks, q, k, v, state_k.reshape(geom.db, nbuf, nk), state_v.reshape(geom.db, nbuf, nk), bias_old, bias_new)

    o_all = jnp.concatenate([o_p, o_s.astype(BF16)], axis=0)
    x = _out_proj(geom, o_all, w_o, b_o, x, modx, 2)

    wbuf = min(WINDOW, geom.s)
    kp = k[:geom.tp].reshape(geom.b, geom.s, kv, hd)[:, geom.s - wbuf:]
    vp = v[:geom.tp].reshape(geom.b, geom.s, kv, hd)[:, geom.s - wbuf:]
    return x, kp, vp, k_s.reshape(geom.db, nbuf, kv, hd), v_s.reshape(geom.db, nbuf, kv, hd)


def _out_proj_body(o_ref, w_ref, b_ref, x_ref, gate_ref, out_ref):
    y = _dot(o_ref[...], w_ref[...]) + b_ref[...]
    out_ref[...] = x_ref[...] + gate_ref[...] * y


def _out_proj_nobias_body(o_ref, w_ref, x_ref, gate_ref, out_ref):
    out_ref[...] = x_ref[...] + gate_ref[...] * _dot(o_ref[...], w_ref[...])


def _out_proj(geom, o_all, w_o, b_o, x, modx, gate_chunk):
    d = geom.d
    w_bf = w_o.astype(BF16)
    if b_o is None:
        body, extra, extra_specs = _out_proj_nobias_body, (), []
    else:
        body, extra, extra_specs = _out_proj_body, (b_o.reshape(1, d),), [pl.BlockSpec((1, d), lambda i: (0, 0))]
    return pl.pallas_call(
        body,
        grid=(geom.nb,),
        in_specs=[geom.row(o_all.shape[1]), geom.full(w_bf)] + extra_specs + [geom.row(d), geom.mod(gate_chunk)],
        out_specs=geom.row(d),
        out_shape=jax.ShapeDtypeStruct((geom.t, d), F32),
        compiler_params=_params(1),
        name="out_proj",
    )(o_all, w_bf, *extra, x, modx)


def _moe_prep_body(x_ref, g_ref, sc_ref, sh_ref, wr_hi_ref, wr_lo_ref, br_ref, h_ref, lg_ref):
    h = _modulate(x_ref[...], g_ref[...], sc_ref[...], sh_ref[...])
    h_hi, h_lo = _split_bf16(h)
    wr_hi = wr_hi_ref[...]
    lg_ref[...] = _dot(h_hi, wr_hi) + _dot(h_hi, wr_lo_ref[...]) + _dot(h_lo, wr_hi) + br_ref[...]
    h_ref[...] = h


def _row_copy(src_hbm, src_row, dst_vmem, dst_row, sem):
    return pltpu.make_async_copy(src_hbm.at[pl.ds(src_row, 1), :], dst_vmem.at[pl.ds(dst_row, 1), :], sem)


def _gather_rows_start(idx_ref, base, n_rows, src_hbm, dst_vmem, sem):
    def body(g, carry):
        for k in range(GATHER_UNROLL):
            r = g * GATHER_UNROLL + k
            _row_copy(src_hbm, idx_ref[base + r], dst_vmem, r, sem).start()
        return carry

    lax.fori_loop(0, n_rows // GATHER_UNROLL, body, 0)


def _gather_rows_wait(n_rows, src_hbm, dst_vmem, sem):
    def body(g, carry):
        for k in range(GATHER_UNROLL):
            _row_copy(src_hbm, 0, dst_vmem, 0, sem).wait()
        return carry

    lax.fori_loop(0, n_rows // GATHER_UNROLL, body, 0)


def _moe_expert_body(be_ref, nvalid_ref, tok_ref, h_hbm, wg_ref, wu_ref, bg_ref, bu_ref, wd_ref, bd_ref, gate_ref,
                     out_ref, xbuf, xbf, sem):
    blk, n = pl.program_id(0), pl.program_id(1)
    last = pl.num_programs(1) - 1
    n_valid = nvalid_ref[0]
    valid = blk < n_valid
    tm = xbf.shape[0]

    @pl.when(n == 0)
    def _():
        @pl.when(blk == 0)
        def _():
            _gather_rows_start(tok_ref, 0, tm, h_hbm, xbuf, sem.at[0])

        @pl.when(valid)
        def _():
            _gather_rows_wait(tm, h_hbm, xbuf, sem.at[0])
            xbf[...] = xbuf[...].astype(BF16)
            out_ref[...] = jnp.broadcast_to(bd_ref[...], out_ref.shape)

        @pl.when(blk + 1 < n_valid)
        def _():
            _gather_rows_start(tok_ref, (blk + 1) * tm, tm, h_hbm, xbuf, sem.at[0])

    @pl.when(valid)
    def _():
        x = xbf[...]
        gl = jnp.minimum(_dot(x, wg_ref[...].astype(BF16)) + bg_ref[...], SWIGLU_LIMIT)
        up = jnp.clip(_dot(x, wu_ref[...].astype(BF16)) + bu_ref[...], -SWIGLU_LIMIT, SWIGLU_LIMIT)
        act = (up + 1.0) * gl * jax.nn.sigmoid(SWIGLU_ALPHA * gl)
        out_ref[...] += _dot(act.astype(BF16), wd_ref[...].astype(BF16))

        @pl.when(n == last)
        def _():
            out_ref[...] *= gate_ref[...]

    @pl.when(jnp.logical_and(jnp.logical_not(valid), n == 0))
    def _():
        out_ref[...] = jnp.zeros_like(out_ref)


def _moe_layer(geom, layer, x, modx, norm_g, w_r, b_r, w_gu, b_gu, w_d, b_d):
    d, t = geom.d, geom.t
    n_exp = w_r.shape[-1]
    de = w_d.shape[2]
    lanes = 128
    e_pad = -(-n_exp // lanes) * lanes
    wr_hi, wr_lo = _split_bf16(jnp.pad(w_r, ((0, 0), (0, e_pad - n_exp))))
    br = jnp.pad(b_r, (0, e_pad - n_exp)).reshape(1, e_pad)

    h, logits = pl.pallas_call(
        _moe_prep_body,
        grid=(geom.nb,),
        in_specs=[geom.row(d), geom.full(norm_g), geom.mod(4), geom.mod(3),
                  geom.full(wr_hi), geom.full(wr_lo), geom.full(br)],
        out_specs=[geom.row(d), geom.row(e_pad)],
        out_shape=[jax.ShapeDtypeStruct((t, d), F32), jax.ShapeDtypeStruct((t, e_pad), F32)],
        compiler_params=_params(1),
        name="moe_prep",
    )(x, norm_g, modx, modx, wr_hi, wr_lo, br)

    tm = EXPERT_ROWS
    top_v, top_e = lax.top_k(logits[:, :n_exp], TOP_K)
    gates = jax.nn.softmax(top_v, axis=-1).reshape(-1)
    e_flat = top_e.reshape(-1).astype(jnp.int32)
    n_assign = t * TOP_K
    order = jnp.argsort(e_flat).astype(jnp.int32)
    rank = jnp.argsort(order).astype(jnp.int32)
    bounds = jnp.searchsorted(e_flat[order], jnp.arange(n_exp + 1, dtype=jnp.int32), side='left').astype(jnp.int32)
    cum, counts = bounds[:-1], bounds[1:] - bounds[:-1]
    padded = (counts + tm - 1) // tm * tm
    pad_end = jnp.cumsum(padded)
    start = pad_end - padded
    n_blocks = -(-(n_assign + n_exp * (tm - 1)) // tm)
    n_slots = n_blocks * tm
    blk_e = jnp.minimum(jnp.searchsorted(pad_end, jnp.arange(n_blocks, dtype=jnp.int32) * tm, side='right'),
                        n_exp - 1).astype(jnp.int32)
    n_valid = (pad_end[-1] // tm).astype(jnp.int32).reshape(1)
    blk_i = jnp.arange(n_blocks, dtype=jnp.int32)
    r_s = (blk_i * tm - start[blk_e])[:, None] + jnp.arange(tm, dtype=jnp.int32)[None, :]
    live = (r_s < counts[blk_e][:, None]) & (blk_i < n_valid[0])[:, None]
    a_s = order[jnp.clip(cum[blk_e][:, None] + r_s, 0, n_assign - 1)]
    tok = jnp.where(live, a_s // TOP_K, 0).astype(jnp.int32).reshape(-1)
    gate = jnp.where(live, gates[a_s], 0.0).reshape(-1)
    slot_of = (start[e_flat] + rank - cum[e_flat]).astype(jnp.int32)

    tn = _pick(de, (EXPERT_COLS, 256, 128))
    nt = de // tn
    out = pl.pallas_call(
        _moe_expert_body,
        grid_spec=pltpu.PrefetchScalarGridSpec(
            num_scalar_prefetch=3,
            grid=(n_blocks, nt),
            in_specs=[pl.BlockSpec(memory_space=pl.ANY),
                      pl.BlockSpec((None, None, d, tn), lambda b, n, be, nv, tk: (layer, be[b], 0, n)),
                      pl.BlockSpec((None, None, d, tn), lambda b, n, be, nv, tk: (layer, be[b], 0, nt + n)),
                      pl.BlockSpec((None, None, 1, tn), lambda b, n, be, nv, tk: (layer, be[b], 0, n)),
                      pl.BlockSpec((None, None, 1, tn), lambda b, n, be, nv, tk: (layer, be[b], 0, nt + n)),
                      pl.BlockSpec((None, None, tn, d), lambda b, n, be, nv, tk: (layer, be[b], n, 0)),
                      pl.BlockSpec((None, None, 1, d), lambda b, n, be, nv, tk: (layer, be[b], 0, 0)),
                      pl.BlockSpec((tm, 1), lambda b, n, be, nv, tk: (b, 0))],
            out_specs=pl.BlockSpec((tm, d), lambda b, n, be, nv, tk: (b, 0)),
            scratch_shapes=[pltpu.VMEM((tm, d), F32), pltpu.VMEM((tm, d), BF16), pltpu.SemaphoreType.DMA((1,))]),
        out_shape=jax.ShapeDtypeStruct((n_slots, d), F32),
        compiler_params=_params(2, gathers=True),
        name="moe_experts",
    )(blk_e, n_valid, tok, h, w_gu, w_gu, b_gu.reshape(*b_gu.shape[:2], 1, -1), b_gu.reshape(*b_gu.shape[:2], 1, -1),
      w_d, b_d.reshape(*b_d.shape[:2], 1, -1), gate.reshape(n_slots, 1))

    return _moe_combine(geom, x, out, slot_of, modx, 5)


def _moe_combine_body(slot_ref, out_hbm, x_ref, gate_ref, o_ref, ybuf, sem, *, top_k):
    i = pl.program_id(0)
    rb = x_ref.shape[0]
    n_rows = top_k * rb
    slot = i % 2

    @pl.when(i == 0)
    def _():
        _gather_rows_start(slot_ref, 0, n_rows, out_hbm, ybuf.at[0], sem.at[0])

    @pl.when(i + 1 < pl.num_programs(0))
    def _():
        _gather_rows_start(slot_ref, (i + 1) * n_rows, n_rows, out_hbm, ybuf.at[1 - slot], sem.at[1 - slot])

    _gather_rows_wait(n_rows, out_hbm, ybuf.at[slot], sem.at[slot])
    y = ybuf[slot, 0:rb, :]
    for k in range(1, top_k):
        y = y + ybuf[slot, k * rb:(k + 1) * rb, :]
    o_ref[...] = x_ref[...] + gate_ref[...] * y


def _moe_combine(geom, x, out, slot_of, modx, gate_chunk):
    d = geom.d
    return pl.pallas_call(
        functools.partial(_moe_combine_body, top_k=TOP_K),
        grid_spec=pltpu.PrefetchScalarGridSpec(
            num_scalar_prefetch=1,
            grid=(geom.nb,),
            in_specs=[pl.BlockSpec(memory_space=pl.ANY),
                      pl.BlockSpec((geom.rb, d), lambda i, sl: (i, 0)),
                      geom.mod(gate_chunk)],
            out_specs=pl.BlockSpec((geom.rb, d), lambda i, sl: (i, 0)),
            scratch_shapes=[pltpu.VMEM((2, TOP_K * geom.rb, d), F32), pltpu.SemaphoreType.DMA((2,))]),
        out_shape=jax.ShapeDtypeStruct((geom.t, d), F32),
        compiler_params=_params(1, gathers=True),
        name="moe_combine",
    )(slot_of.reshape(geom.nb, geom.rb, TOP_K).transpose(0, 2, 1).reshape(-1), out, x, modx)


def _rope_tables(pos, half):
    inv = ROPE_THETA ** (-jnp.arange(half, dtype=F32) / half)
    ang = pos.astype(F32)[:, None] * inv
    return jnp.cos(ang), jnp.sin(ang)


def _mla_down_body(x_ref, g_ref, sc_ref, sh_ref, w_ref, gqa_ref, gkva_ref,
                   cq_ref, ckv_ref, kr_ref, krs_ref, *, ql, kl, rope):
    h = _modulate(x_ref[...], g_ref[...], sc_ref[...], sh_ref[...]).astype(BF16)
    dd = _dot(h, w_ref[...])
    dq = dd[:, :ql]
    cq_ref[...] = (dq * lax.rsqrt(jnp.mean(dq * dq, axis=-1, keepdims=True) + NORM_EPS) * gqa_ref[...]).astype(BF16)
    dk = dd[:, ql:ql + kl]
    ckv_ref[...] = dk * lax.rsqrt(jnp.mean(dk * dk, axis=-1, keepdims=True) + NORM_EPS) * gkva_ref[...]
    kr_ref[...] = dd[:, ql + kl:ql + kl + rope]
    krs_ref[...] = dd[:, ql + kl + rope:]


def _mla_q_prompt_body(cq_ref, w_ref, a_ref, b_ref, q_ref, *, heads, hw, qk):
    cq = cq_ref[...]
    a, bt = a_ref[...], b_ref[...]
    for h in range(heads):
        x = _dot(cq, w_ref[:, h * hw:(h + 1) * hw])
        xs = _dot(cq, w_ref[:, (heads + h) * hw:(heads + h + 1) * hw])
        r = lax.rsqrt(jnp.sum(x * x, axis=-1, keepdims=True) * (1.0 / qk) + NORM_EPS)
        q_ref[:, h * hw:(h + 1) * hw] = ((x * a + xs * bt) * r).astype(BF16)


def _mla_q_sample_body(cq_ref, w_ref, a_ref, b_ref, a2_ref, b2_ref, wuk_ref, qabs_ref, qr_ref,
                       *, heads, hw, qk, nope, rope, nq_tok):
    cq = cq_ref[...]
    rb = cq.shape[0]
    nb = rb // nq_tok
    a, bt = a_ref[...], b_ref[...]
    a2, b2 = a2_ref[...], b2_ref[...]
    for h in range(heads):
        x = _dot(cq, w_ref[:, h * hw:(h + 1) * hw])
        xs = _dot(cq, w_ref[:, (heads + h) * hw:(heads + h + 1) * hw])
        r = lax.rsqrt(jnp.sum(x * x, axis=-1, keepdims=True) * (1.0 / qk) + NORM_EPS)
        qn = (x[:, :nope] * a[:, :nope] * r).astype(BF16)
        qabs = _dot_nt(qn, wuk_ref[:, h * nope:(h + 1) * nope])
        xr, xsr = x[:, nope:nope + rope], xs[:, nope:nope + rope]
        q1 = (xr * a[:, nope:nope + rope] + xsr * bt[:, nope:nope + rope]) * r
        q2 = (xsr * a2 + xr * b2) * r
        rows = slice(h * nq_tok, (h + 1) * nq_tok)
        qabs_ref[:, rows, :] = qabs.reshape(nb, nq_tok, qabs.shape[1])
        qr_ref[:, rows, :rope] = q1.reshape(nb, nq_tok, rope)
        qr_ref[:, rows, rope:] = q2.reshape(nb, nq_tok, rope)


def _mla_kv_prompt_body(ckv_ref, kr_ref, krs_ref, wuk_ref, wuv_ref, gkn_ref, ak_ref, bk_ref, k_ref, v_ref,
                        *, heads, hw, qk, nope, rope, vd):
    c = ckv_ref[...].astype(BF16)
    kr = kr_ref[...]
    ssr = jnp.sum(kr * kr, axis=-1, keepdims=True)
    krot = kr * ak_ref[...] + krs_ref[...] * bk_ref[...]
    gkn = gkn_ref[...]
    zeros = jnp.zeros((c.shape[0], hw - nope - rope), BF16)
    for h in range(heads):
        kn = _dot(c, wuk_ref[:, h * nope:(h + 1) * nope])
        r = lax.rsqrt((jnp.sum(kn * kn, axis=-1, keepdims=True) + ssr) * (1.0 / qk) + NORM_EPS)
        k_ref[:, h * hw:h * hw + nope] = (kn * gkn * r).astype(BF16)
        k_ref[:, h * hw + nope:h * hw + nope + rope] = (krot * r).astype(BF16)
        k_ref[:, h * hw + nope + rope:(h + 1) * hw] = zeros
        v_ref[:, h * vd:(h + 1) * vd] = _dot(c, wuv_ref[:, h * vd:(h + 1) * vd]).astype(BF16)


def _flash_body(qi_ref, kj_ref, q_ref, k_ref, v_ref, o_ref, m_ref, l_ref, acc_ref):
    n = pl.program_id(1)
    qi, kj = qi_ref[n], kj_ref[n]
    tq, tk = q_ref.shape[0], k_ref.shape[0]

    @pl.when(kj == 0)
    def _():
        m_ref[...] = jnp.full_like(m_ref, MASKED)
        l_ref[...] = jnp.zeros_like(l_ref)
        acc_ref[...] = jnp.zeros_like(acc_ref)

    k = k_ref[...]
    v = v_ref[...]
    cols = kj * tk + lax.broadcasted_iota(jnp.int32, (1, tk), 1)
    sub = min(tq, FLASH_SUB_ROWS)
    for r0 in range(0, tq, sub):
        rs = slice(r0, r0 + sub)
        s = _dot_nt(q_ref[rs, :], k)
        rows = qi * tq + r0 + lax.broadcasted_iota(jnp.int32, (sub, 1), 0)
        s = jnp.where(cols <= rows, s, MASKED)
        m_old = m_ref[rs, :]
        m_new = jnp.maximum(m_old, jnp.max(s, axis=-1, keepdims=True))
        p = jnp.exp(s - m_new)
        alpha = jnp.exp(m_old - m_new)
        l_ref[rs, :] = l_ref[rs, :] * alpha + jnp.sum(p, axis=-1, keepdims=True)
        acc_ref[rs, :] = acc_ref[rs, :] * alpha + _dot(p.astype(BF16), v)
        m_ref[rs, :] = m_new

    @pl.when(kj == qi)
    def _():
        o_ref[...] = (acc_ref[...] * (1.0 / l_ref[...])).astype(BF16)


def _mla_prompt_attn(q_p, k_p, v_p, seq_index, s, heads, hw, vd):
    tq = _pick(s, (FLASH_ROWS, 512, 256, 128, 64))
    nqb = s // tq
    pairs = [(i, j) for i in range(nqb) for j in range(i + 1)]
    qi_tab = jnp.asarray([p[0] for p in pairs], jnp.int32)
    kj_tab = jnp.asarray([p[1] for p in pairs], jnp.int32)
    off = seq_index * nqb
    return pl.pallas_call(
        _flash_body,
        grid_spec=pltpu.PrefetchScalarGridSpec(
            num_scalar_prefetch=2,
            grid=(heads, len(pairs)),
            in_specs=[pl.BlockSpec((tq, hw), lambda h, n, qi, kj: (off + qi[n], h)),
                      pl.BlockSpec((tq, hw), lambda h, n, qi, kj: (off + kj[n], h)),
                      pl.BlockSpec((tq, vd), lambda h, n, qi, kj: (off + kj[n], h))],
            out_specs=pl.BlockSpec((tq, vd), lambda h, n, qi, kj: (qi[n], h)),
            scratch_shapes=[pltpu.VMEM((tq, 1), F32), pltpu.VMEM((tq, 1), F32), pltpu.VMEM((tq, vd), F32)]),
        out_shape=jax.ShapeDtypeStruct((s, heads * vd), BF16),
        compiler_params=_params(2),
        name="mla_prompt_attn",
    )(qi_tab, kj_tab, q_p, k_p, v_p)


def _mla_sample_body(pt_ref, qabs_ref, qr_ref, cnew_ref, krnew_ref, cs_ref, csn_ref, wukt_ref, ckv_hbm, krt_hbm,
                     out_ref, cbuf, krbuf, sem_c, sem_k, qabs_bf, qr_bf, m_ref, l_ref, acc_ref,
                     *, layer, heads, nq_tok, nope, qk, pages, chunk_pages):
    b, j = pl.program_id(0), pl.program_id(1)
    nj = pl.num_programs(1)
    step = b * nj + j
    last_step = pl.num_programs(0) * nj - 1
    slot = step % 2
    hq = heads * nq_tok
    page = cbuf.shape[2]

    def page_copies(step_, slot_, u):
        phys = pt_ref[step_ * pages + u]
        return (pltpu.make_async_copy(ckv_hbm.at[layer, phys], cbuf.at[slot_, u], sem_c.at[slot_]),
                pltpu.make_async_copy(krt_hbm.at[layer, phys], krbuf.at[slot_, u], sem_k.at[slot_]))

    def start_pages(step_, slot_, us):
        for u in us:
            for cp in page_copies(step_, slot_, u):
                cp.start()

    def wait_pages(step_, slot_):
        for u in range(pages):
            for cp in page_copies(step_, slot_, u):
                cp.wait()

    @pl.when(step == 0)
    def _():
        start_pages(0, 0, range(pages))

    wait_pages(step, slot)
    next_step = jnp.minimum(step + 1, last_step)

    @pl.when(j == 0)
    def _():
        m_ref[...] = jnp.full_like(m_ref, MASKED)
        l_ref[...] = jnp.zeros_like(l_ref)
        acc_ref[...] = jnp.zeros_like(acc_ref)
        qabs_bf[...] = qabs_ref[...].astype(BF16)
        qr_bf[...] = qr_ref[...].astype(BF16)

    def scores(c_bf, kr_t, cs):
        n = c_bf.shape[0]
        ssn_parts = []
        for h0 in range(0, heads, SAMPLE_HEAD_GROUP):
            kt = _dot_nt(wukt_ref[h0 * nope:(h0 + SAMPLE_HEAD_GROUP) * nope, :], c_bf)
            ssn_parts.append(jnp.sum((kt * kt).reshape(SAMPLE_HEAD_GROUP, nope, n), axis=1))
        ssn = jnp.concatenate(ssn_parts, axis=0)
        ssr = jnp.sum(kr_t * kr_t, axis=0, keepdims=True)
        rk = lax.rsqrt((ssn + ssr) * (1.0 / qk) + NORM_EPS)
        k_rot = (jnp.concatenate([kr_t, kr_t], axis=0) * cs).astype(BF16)
        s = _dot_nt(qabs_bf[...], c_bf) + _dot(qr_bf[...], k_rot)
        return s * jnp.broadcast_to(rk[:, None, :], (heads, nq_tok, n)).reshape(hq, n)

    def update(s, c_bf):
        m_old = m_ref[...]
        m_new = jnp.maximum(m_old, jnp.max(s, axis=-1, keepdims=True))
        p = jnp.exp(s - m_new)
        alpha = jnp.exp(m_old - m_new)
        l_ref[...] = l_ref[...] * alpha + jnp.sum(p, axis=-1, keepdims=True)
        acc_ref[...] = acc_ref[...] * alpha + _dot(p.astype(BF16), c_bf)
        m_ref[...] = m_new

    s_parts, c_parts = [], []
    for u in range(0, pages, chunk_pages):
        us = range(u, u + chunk_pages)
        c_bf = jnp.concatenate([cbuf[slot, v].astype(BF16) for v in us], axis=0)
        kr_t = jnp.concatenate([krbuf[slot, v] for v in us], axis=1)
        s_parts.append(scores(c_bf, kr_t, cs_ref[:, u * page:(u + chunk_pages) * page]))
        c_parts.append(c_bf)
        start_pages(next_step, 1 - slot, us)
    update(jnp.concatenate(s_parts, axis=1), jnp.concatenate(c_parts, axis=0))

    @pl.when(j == nj - 1)
    def _():
        n_new = cnew_ref.shape[0]
        row_q = lax.broadcasted_iota(jnp.int32, (hq, n_new), 0) % nq_tok
        col = lax.broadcasted_iota(jnp.int32, (hq, n_new), 1)
        c_bf = cnew_ref[...].astype(BF16)
        s = scores(c_bf, krnew_ref[...], csn_ref[...])
        update(jnp.where(col <= row_q, s, MASKED), c_bf)
        out_ref[...] = acc_ref[...] * (1.0 / l_ref[...])

    @pl.when(step == last_step)
    def _():
        wait_pages(last_step, 1 - slot)


def _mla_sample_attn(layer, page_table, qabs, qr, cnew, krnew_t, cache_ckv, cache_kr_t, cs_old, cs_new, w_uk_t,
                     *, heads, nq_tok, nope, qk):
    db, hq, kl = qabs.shape
    n_pages = page_table.shape[1]
    page = cache_ckv.shape[2]
    rope = cache_kr_t.shape[2]
    n_new = cnew.shape[1]
    pg = _pick(n_pages, (PAGES_PER_STEP, 8, 4, 2, 1))
    chunk_pages = _pick(pg, (SAMPLE_CHUNK_PAGES, 2, 1))
    return pl.pallas_call(
        functools.partial(_mla_sample_body, layer=layer, heads=heads, nq_tok=nq_tok, nope=nope, qk=qk, pages=pg,
                          chunk_pages=chunk_pages),
        grid_spec=pltpu.PrefetchScalarGridSpec(
            num_scalar_prefetch=1,
            grid=(db, n_pages // pg),
            in_specs=[pl.BlockSpec((None, hq, kl), lambda b, j, pt: (b, 0, 0)),
                      pl.BlockSpec((None, hq, 2 * rope), lambda b, j, pt: (b, 0, 0)),
                      pl.BlockSpec((None, n_new, kl), lambda b, j, pt: (b, 0, 0)),
                      pl.BlockSpec((None, rope, n_new), lambda b, j, pt: (b, 0, 0)),
                      pl.BlockSpec((2 * rope, pg * page), lambda b, j, pt: (0, j)),
                      pl.BlockSpec((2 * rope, n_new), lambda b, j, pt: (0, 0)),
                      pl.BlockSpec(w_uk_t.shape, lambda b, j, pt: (0, 0)),
                      pl.BlockSpec(memory_space=pl.ANY),
                      pl.BlockSpec(memory_space=pl.ANY)],
            out_specs=pl.BlockSpec((None, hq, kl), lambda b, j, pt: (b, 0, 0)),
            scratch_shapes=[pltpu.VMEM((2, pg, page, kl), F32), pltpu.VMEM((2, pg, rope, page), F32),
                            pltpu.SemaphoreType.DMA((2,)), pltpu.SemaphoreType.DMA((2,)),
                            pltpu.VMEM((hq, kl), BF16), pltpu.VMEM((hq, 2 * rope), BF16),
                            pltpu.VMEM((hq, 1), F32), pltpu.VMEM((hq, 1), F32), pltpu.VMEM((hq, kl), F32)]),
        out_shape=jax.ShapeDtypeStruct((db, hq, kl), F32),
        compiler_params=_params(2, gathers=True),
        name="mla_sample_attn",
    )(page_table.reshape(-1), qabs, qr, cnew, krnew_t, cs_old, cs_new, w_uk_t, cache_ckv, cache_kr_t)


def _mla_uv_body(lat_ref, wuv_ref, o_ref, *, heads, nq_tok, vd):
    nb, _, kl = lat_ref.shape
    for h in range(heads):
        lat = lat_ref[:, h * nq_tok:(h + 1) * nq_tok, :].reshape(nb * nq_tok, kl).astype(BF16)
        o_ref[:, h * vd:(h + 1) * vd] = _dot(lat, wuv_ref[:, h * vd:(h + 1) * vd])


def _mla_layer(geom, layer, x, modx, norm_g, w_dkv, g_qa, g_kva, w_uq, g_q, w_uk, g_k, w_uv, w_o,
               cache_ckv, cache_kr, page_table, past_len):
    d, rb, qt = geom.d, geom.rb, geom.q
    ql, kl = g_qa.shape[0], g_kva.shape[0]
    rope = cache_kr.shape[-1]
    half = rope // 2
    qk = g_q.shape[0]
    nope = qk - rope
    heads = w_uq.shape[1] // qk
    vd = w_uv.shape[1] // heads
    hw = MXU_WIDTH
    scale = qk ** -0.5
    page = cache_ckv.shape[2]
    n_pages = page_table.shape[1]
    tp, ts = geom.tp, geom.ts

    w_dn = jnp.concatenate([w_dkv, w_dkv[:, ql + kl + half:], w_dkv[:, ql + kl:ql + kl + half]], axis=1).astype(BF16)
    gqa, gkva = g_qa.reshape(1, ql), g_kva.reshape(1, kl)
    cq, ckv, kr, krs = pl.pallas_call(
        functools.partial(_mla_down_body, ql=ql, kl=kl, rope=rope),
        grid=(geom.nb,),
        in_specs=[geom.row(d), geom.full(norm_g), geom.mod(1), geom.mod(0), geom.full(w_dn),
                  geom.full(gqa), geom.full(gkva)],
        out_specs=[geom.row(ql), geom.row(kl), geom.row(rope), geom.row(rope)],
        out_shape=[jax.ShapeDtypeStruct((geom.t, ql), BF16), jax.ShapeDtypeStruct((geom.t, kl), F32),
                   jax.ShapeDtypeStruct((geom.t, rope), F32), jax.ShapeDtypeStruct((geom.t, rope), F32)],
        compiler_params=_params(1),
        name="mla_down",
    )(x, norm_g, modx, modx, w_dn, gqa, gkva)

    w3 = w_uq.reshape(ql, heads, qk)
    wn, w1, w2 = w3[:, :, :nope], w3[:, :, nope:nope + half], w3[:, :, nope + half:]
    zpad = jnp.zeros((ql, heads, hw - qk), F32)
    w_q = jnp.concatenate([jnp.concatenate([wn, w1, w2, zpad], axis=-1).reshape(ql, heads * hw),
                           jnp.concatenate([jnp.zeros_like(wn), w2, w1, zpad], axis=-1).reshape(ql, heads * hw)],
                          axis=1).astype(BF16)
    gqn, gq1, gq2 = g_q[:nope], g_q[nope:nope + half], g_q[nope + half:]
    gkn, gk1, gk2 = g_k[:nope], g_k[nope:nope + half], g_k[nope + half:]

    def q_tables(pos, nope_gain):
        cos, sin = _rope_tables(pos, half)
        n = pos.shape[0]
        z = jnp.zeros((n, hw - qk), F32)
        a = jnp.concatenate([jnp.broadcast_to(nope_gain, (n, nope)), gq1 * cos, gq2 * cos, z], axis=1) * scale
        b = jnp.concatenate([jnp.zeros((n, nope), F32), -gq2 * sin, gq1 * sin, z], axis=1) * scale
        return a, b, cos, sin

    pos_p = jnp.tile(jnp.arange(geom.s, dtype=jnp.int32), geom.b)
    pos_s = jnp.tile(past_len + jnp.arange(qt, dtype=jnp.int32), geom.db)
    a_p, b_p, cos_p, sin_p = q_tables(pos_p, gqn)
    a_s, b_s, cos_s, sin_s = q_tables(pos_s, gqn * gkn)
    a2_s = jnp.concatenate([gq2 * cos_s, -gq1 * cos_s], axis=1) * scale
    b2_s = jnp.concatenate([gq1 * sin_s, gq2 * sin_s], axis=1) * scale

    q_p = pl.pallas_call(
        functools.partial(_mla_q_prompt_body, heads=heads, hw=hw, qk=qk),
        grid=(geom.nbp,),
        in_specs=[geom.row(ql), geom.full(w_q), geom.row(hw), geom.row(hw)],
        out_specs=geom.row(heads * hw),
        out_shape=jax.ShapeDtypeStruct((tp, heads * hw), BF16),
        compiler_params=_params(1),
        name="mla_q_prompt",
    )(cq, w_q, a_p, b_p)

    w_uk_bf = w_uk.astype(BF16)
    nbq = rb // qt
    hq = heads * qt
    qabs, qr = pl.pallas_call(
        functools.partial(_mla_q_sample_body, heads=heads, hw=hw, qk=qk, nope=nope, rope=rope, nq_tok=qt),
        grid=(geom.nbs,),
        in_specs=[geom.row(ql, geom.nbp), geom.full(w_q), geom.row(hw), geom.row(hw), geom.row(rope), geom.row(rope),
                  geom.full(w_uk_bf)],
        out_specs=[pl.BlockSpec((nbq, hq, kl), lambda i: (i, 0, 0)),
                   pl.BlockSpec((nbq, hq, 2 * rope), lambda i: (i, 0, 0))],
        out_shape=[jax.ShapeDtypeStruct((geom.db, hq, kl), F32), jax.ShapeDtypeStruct((geom.db, hq, 2 * rope), F32)],
        compiler_params=_params(1),
        name="mla_q_sample",
    )(cq, w_q, a_s, b_s, a2_s, b2_s, w_uk_bf)

    w_uv_bf = w_uv.astype(BF16)
    a_k = jnp.concatenate([gk1 * cos_p, gk2 * cos_p], axis=1)
    b_k = jnp.concatenate([-gk2 * sin_p, gk1 * sin_p], axis=1)
    gkn_row = gkn.reshape(1, nope)
    k_p, v_p = pl.pallas_call(
        functools.partial(_mla_kv_prompt_body, heads=heads, hw=hw, qk=qk, nope=nope, rope=rope, vd=vd),
        grid=(geom.nbp,),
        in_specs=[geom.row(kl), geom.row(rope), geom.row(rope), geom.full(w_uk_bf), geom.full(w_uv_bf),
                  geom.full(gkn_row), geom.row(rope), geom.row(rope)],
        out_specs=[geom.row(heads * hw), geom.row(heads * vd)],
        out_shape=[jax.ShapeDtypeStruct((tp, heads * hw), BF16), jax.ShapeDtypeStruct((tp, heads * vd), BF16)],
        compiler_params=_params(1),
        name="mla_kv_prompt",
    )(ckv, kr, krs, w_uk_bf, w_uv_bf, gkn_row, a_k, b_k)

    o_parts = [_mla_prompt_attn(q_p, k_p, v_p, bi, geom.s, heads, hw, vd) for bi in range(geom.b)]

    n_new = 128
    ckv_s = ckv[tp:].reshape(geom.db, qt, kl)
    kr_s = kr[tp:].reshape(geom.db, qt, rope)
    cnew = jnp.pad(ckv_s, ((0, 0), (0, n_new - qt), (0, 0)))
    krnew_t = jnp.pad(kr_s, ((0, 0), (0, n_new - qt), (0, 0))).transpose(0, 2, 1)
    cos_c, sin_c = _rope_tables(jnp.arange(past_len + n_new, dtype=jnp.int32), half)
    cs = jnp.concatenate([gk1 * cos_c, gk2 * cos_c, gk1 * sin_c, gk2 * sin_c], axis=1).T
    o_lat = _mla_sample_attn(layer, page_table, qabs, qr, cnew, krnew_t, cache_ckv, cache_kr.transpose(0, 1, 3, 2),
                             cs[:, :past_len], cs[:, past_len:], w_uk.T.astype(BF16),
                             heads=heads, nq_tok=qt, nope=nope, qk=qk)

    o_s = pl.pallas_call(
        functools.partial(_mla_uv_body, heads=heads, nq_tok=qt, vd=vd),
        grid=(geom.nbs,),
        in_specs=[pl.BlockSpec((nbq, hq, kl), lambda i: (i, 0, 0)), geom.full(w_uv_bf)],
        out_specs=geom.row(heads * vd),
        out_shape=jax.ShapeDtypeStruct((ts, heads * vd), F32),
        compiler_params=_params(1),
        name="mla_uv",
    )(o_lat, w_uv_bf)

    o_all = jnp.concatenate(o_parts + [o_s.astype(BF16)], axis=0)
    x = _out_proj(geom, o_all, w_o, None, x, modx, 2)
    return (x, ckv[:tp].reshape(geom.b, geom.s, kl), kr[:tp].reshape(geom.b, geom.s, rope), ckv_s, kr_s)


def kernel(x_prompt, x_sample, state_swa_k, state_swa_v, cache_mla_ckv, cache_mla_kr, page_table, c_prompt, c_sample, ada_w, ada_b, norm_attn_g, norm_ffn_g, swa_w_qkv, swa_b_qkv, swa_g_q, swa_g_k, swa_sinks, swa_w_o, swa_b_o, mla_w_dkv, mla_g_qa, mla_g_kva, mla_w_uq, mla_g_q, mla_w_uk, mla_g_k, mla_w_uv, mla_w_o, moe_w_router, moe_b_router, moe_w_gate_up, moe_b_gate_up, moe_w_down, moe_b_down):
    b, s, d = x_prompt.shape
    db, qt, _ = x_sample.shape
    depth = ada_w.shape[0]
    past_len = page_table.shape[1] * cache_mla_ckv.shape[2]
    geom = _Geom(b, s, db, qt, d)

    n_c = b + db
    m_pad = -(-n_c // 8) * 8
    c_all = jnp.pad(jnp.concatenate([c_prompt, c_sample], axis=0), ((0, m_pad - n_c), (0, 0)))
    mod = _adaln(c_all, ada_w, ada_b)

    x = jnp.concatenate([x_prompt.reshape(b * s, d), x_sample.reshape(db * qt, d)], axis=0)
    swa_kp, swa_vp, swa_ks, swa_vs = [], [], [], []
    ckv_p, kr_p, ckv_s, kr_s = [], [], [], []
    n_mixers = 2
    for i in range(depth):
        modx = geom.expand_mod(mod[i])
        g_attn = norm_attn_g[i].reshape(1, d)
        g_ffn = norm_ffn_g[i].reshape(1, d)
        j = i // n_mixers
        if i % n_mixers == 0:
            x, kp, vp, ks, vs = _swa_layer(geom, x, modx, g_attn, swa_w_qkv[j], swa_b_qkv[j], swa_g_q[j], swa_g_k[j],
                                           swa_sinks[j], swa_w_o[j], swa_b_o[j], state_swa_k[j], state_swa_v[j], past_len)
            swa_kp.append(kp)
            swa_vp.append(vp)
            swa_ks.append(ks)
            swa_vs.append(vs)
        else:
            x, cp, rp, cs, rs = _mla_layer(geom, j, x, modx, g_attn, mla_w_dkv[j], mla_g_qa[j], mla_g_kva[j], mla_w_uq[j],
                                           mla_g_q[j], mla_w_uk[j], mla_g_k[j], mla_w_uv[j], mla_w_o[j],
                                           cache_mla_ckv, cache_mla_kr, page_table, past_len)
            ckv_p.append(cp)
            kr_p.append(rp)
            ckv_s.append(cs)
            kr_s.append(rs)
        x = _moe_layer(geom, i, x, modx, g_ffn, moe_w_router[i], moe_b_router[i], moe_w_gate_up, moe_b_gate_up,
                       moe_w_down, moe_b_down)

    y_prompt = x[:b * s].reshape(b, s, d)
    y_sample = x[b * s:].reshape(db, qt, d)
    return (y_prompt, y_sample, jnp.stack(swa_kp), jnp.stack(swa_vp), jnp.stack(swa_ks), jnp.stack(swa_vs),
            jnp.stack(ckv_p), jnp.stack(kr_p), jnp.stack(ckv_s), jnp.stack(kr_s))
```

```python
import functools

import jax
import jax.numpy as jnp
from jax import lax
from jax.experimental import pallas as pl
from jax.experimental.pallas import tpu as pltpu

F32 = jnp.float32
BF16 = jnp.bfloat16

WINDOW = 128
ALIBI_MAX_BIAS = 8.0
ROPE_THETA = 10000.0
TOP_K = 4
SWIGLU_LIMIT = 7.0
SWIGLU_ALPHA = 1.702
NORM_EPS = 1e-6

MASKED = -1e30
VMEM_LIMIT_BYTES = 56 * 1024 * 1024
MXU_WIDTH = 256
EXPERT_ROWS = 512
EXPERT_COLS = 512
GATHER_UNROLL = 8
PAGES_PER_STEP = 32
FLASH_ROWS = 2048
FLASH_SUB_ROWS = 256
SAMPLE_CHUNK_PAGES = 2
SAMPLE_HEAD_GROUP = 16


def _pick(n, prefs):
    for p in prefs:
        if n % p == 0:
            return p
    raise ValueError(f"no tile in {prefs} divides {n}")


def _dot(a, b):
    return jnp.dot(a, b, preferred_element_type=F32)


def _dot_nt(a, b):
    return lax.dot_general(a, b, (((1,), (1,)), ((), ())), preferred_element_type=F32)


def _split_bf16(x):
    hi = x.astype(BF16)
    lo = (x - hi.astype(F32)).astype(BF16)
    return hi, lo


def _params(n_axes, gathers=False):
    return pltpu.CompilerParams(dimension_semantics=("arbitrary",) * n_axes,
                                vmem_limit_bytes=VMEM_LIMIT_BYTES, disable_bounds_checks=gathers)


def _modulate(x, g, scale, shift):
    y = x * lax.rsqrt(jnp.mean(x * x, axis=-1, keepdims=True) + NORM_EPS) * g
    return y * (1.0 + scale) + shift


def _adaln_body(c_ref, w_ref, b_ref, o_ref):
    c = c_ref[...]
    a = (c * jax.nn.sigmoid(c)).astype(BF16)
    o_ref[...] = _dot(a, w_ref[...].astype(BF16)) + b_ref[...]


def _adaln(c_all, ada_w, ada_b):
    depth, d, n = ada_w.shape
    m = c_all.shape[0]
    tn = _pick(n, (1024, 512, 256, 128))
    return pl.pallas_call(
        _adaln_body,
        grid=(depth, n // tn),
        in_specs=[pl.BlockSpec((m, d), lambda l, j: (0, 0)),
                  pl.BlockSpec((None, d, tn), lambda l, j: (l, 0, j)),
                  pl.BlockSpec((None, 1, tn), lambda l, j: (l, 0, j))],
        out_specs=pl.BlockSpec((None, m, tn), lambda l, j: (l, 0, j)),
        out_shape=jax.ShapeDtypeStruct((depth, m, n), F32),
        compiler_params=_params(2),
        name="adaln",
    )(c_all, ada_w, ada_b.reshape(depth, 1, n))


class _Geom:
    def __init__(self, b, s, db, q, d):
        self.b, self.s, self.db, self.q, self.d = b, s, db, q, d
        self.tp, self.ts = b * s, db * q
        self.t = self.tp + self.ts
        rb = 256
        while s % rb or self.ts % rb or rb % q:
            rb //= 2
            if rb < 8:
                raise ValueError("token counts must be multiples of 8")
        self.rb = rb
        self.nbp, self.nbs = self.tp // rb, self.ts // rb
        self.nb = self.nbp + self.nbs

    def row(self, width, offset=0):
        return pl.BlockSpec((self.rb, width), lambda i: (i + offset, 0))

    def full(self, arr):
        nd = arr.ndim
        return pl.BlockSpec(arr.shape, lambda i: (0,) * nd)

    def mod(self, chunk, offset=0):
        nbp, spb, b = self.nbp, self.s // self.rb, self.b

        def index(i, *_):
            i = i + offset
            return (jnp.where(i < nbp, i // spb, b + i - nbp), chunk)

        return pl.BlockSpec((self.rb, self.d), index)

    def expand_mod(self, mod):
        return jnp.concatenate([jnp.repeat(mod[:self.b], self.rb, axis=0),
                                jnp.repeat(mod[self.b:self.b + self.db], self.q, axis=0)], axis=0)


def _swa_proj_body(x_ref, g_ref, sc_ref, sh_ref, w_ref, b_ref, gq_ref, gk_ref, e_ref,
                   q_ref, k_ref, v_ref, vt_ref, *, nq, nk, hd):
    h = _modulate(x_ref[...], g_ref[...], sc_ref[...], sh_ref[...]).astype(BF16)
    qkv = _dot(h, w_ref[...]) + b_ref[...]
    e = e_ref[...]
    inv_hd = 1.0 / hd
    for c in range(nq // nk):
        qc = qkv[:, c * nk:(c + 1) * nk]
        ss = _dot((qc * qc).astype(BF16), e)
        q_ref[:, c * nk:(c + 1) * nk] = (qc * lax.rsqrt(ss * inv_hd + NORM_EPS) * gq_ref[...]).astype(BF16)
    kc = qkv[:, nq:nq + nk]
    ss = _dot((kc * kc).astype(BF16), e)
    k_ref[...] = kc * lax.rsqrt(ss * inv_hd + NORM_EPS) * gk_ref[...]
    v = qkv[:, nq + nk:]
    v_ref[...] = v
    vt_ref[...] = v.T


def _swa_prompt_body(sink_ref, q_ref, kc_ref, kp_ref, vtc_ref, vtp_ref, bias_ref, o_ref, ot_ref, *, kv, grp, hd):
    i = pl.program_id(0)
    w = q_ref.shape[0]
    lanes = 2 * hd
    key_row = lax.broadcasted_iota(jnp.int32, (2 * w, 1), 0)
    no_prev = jnp.where(jnp.logical_and(i == 0, key_row < w), MASKED, 0.0)
    lane = lax.broadcasted_iota(jnp.int32, (1, lanes), 1)
    for g in range(kv):
        tile = slice((g // 2) * lanes, (g // 2 + 1) * lanes)
        kcat = jnp.concatenate([kp_ref[:, tile], kc_ref[:, tile]], axis=0)
        own = (lane >= hd) if g % 2 else (lane < hd)
        k_here = jnp.where(own, kcat, 0.0)
        k_at = [None, None]
        k_at[g % 2] = k_here.astype(BF16)
        k_at[1 - g % 2] = pltpu.roll(k_here, hd, axis=1).astype(BF16)
        rows = slice(g * hd, (g + 1) * hd)
        vt = jnp.concatenate([vtp_ref[rows, :], vtc_ref[rows, :]], axis=1).astype(BF16)
        for j in range(grp):
            h = g * grp + j
            q_pair = q_ref[:, (h // 2) * lanes:(h // 2 + 1) * lanes]
            s = _dot_nt(k_at[h % 2], q_pair) + bias_ref[h] + no_prev
            sink = sink_ref[h]
            m = jnp.maximum(jnp.max(s, axis=0, keepdims=True), sink)
            p = jnp.exp(s - m)
            den = jnp.sum(p, axis=0, keepdims=True) + jnp.exp(sink - m)
            ot_ref[h * hd:(h + 1) * hd, :] = _dot(vt, (p * (1.0 / den)).astype(BF16))
    o_ref[...] = ot_ref[...].T.astype(BF16)


def _swa_sample_body(sink_ref, q_ref, kn_ref, vn_ref, ks_ref, vs_ref, bias_old_ref, bias_new_ref,
                     o_ref, ko_ref, vo_ref, *, kv, grp, hd, nq_tok):
    nb, nbuf, _ = ks_ref.shape
    qt = nq_tok
    qf = q_ref[...].astype(F32)
    pad = jnp.zeros((nbuf - qt, hd), F32)
    for b in range(nb):
        rows = slice(b * qt, (b + 1) * qt)
        knew, vnew = kn_ref[rows, :], vn_ref[rows, :]
        ko_ref[b, :nbuf - qt, :] = ks_ref[b, qt:, :]
        ko_ref[b, nbuf - qt:, :] = knew
        vo_ref[b, :nbuf - qt, :] = vs_ref[b, qt:, :]
        vo_ref[b, nbuf - qt:, :] = vnew
        for g in range(kv):
            cs = slice(g * hd, (g + 1) * hd)
            k_old = ks_ref[b, :, cs].astype(BF16)
            v_old = vs_ref[b, :, cs].astype(BF16)
            k_new = jnp.concatenate([knew[:, cs], pad], axis=0).astype(BF16)
            v_new = jnp.concatenate([vnew[:, cs], pad], axis=0).astype(BF16)
            qg = jnp.concatenate([qf[rows, (g * grp + j) * hd:(g * grp + j + 1) * hd] for j in range(grp)],
                                 axis=0).astype(BF16)
            s_old = _dot_nt(qg, k_old) + bias_old_ref[g]
            s_new = _dot_nt(qg, k_new) + bias_new_ref[g]
            sink = jnp.concatenate([jnp.full((qt, 1), sink_ref[g * grp + j], F32) for j in range(grp)], axis=0)
            m = jnp.maximum(jnp.maximum(jnp.max(s_old, axis=-1, keepdims=True),
                                        jnp.max(s_new, axis=-1, keepdims=True)), sink)
            p_old = jnp.exp(s_old - m)
            p_new = jnp.exp(s_new - m)
            den = (jnp.sum(p_old, axis=-1, keepdims=True) + jnp.sum(p_new, axis=-1, keepdims=True)
                   + jnp.exp(sink - m))
            inv = 1.0 / den
            og = _dot((p_old * inv).astype(BF16), v_old) + _dot((p_new * inv).astype(BF16), v_new)
            for j in range(grp):
                o_ref[rows, (g * grp + j) * hd:(g * grp + j + 1) * hd] = og[j * qt:(j + 1) * qt, :]


def _alibi_bias(dist, valid, heads, kv):
    slopes = jnp.exp2(-ALIBI_MAX_BIAS * jnp.arange(1, heads + 1, dtype=F32) / heads)
    bias = jnp.where(valid[None], -slopes[:, None, None] * dist[None].astype(F32), MASKED)
    return bias.reshape(kv, (heads // kv) * dist.shape[0], dist.shape[1])


def _swa_layer(geom, x, modx, norm_g, w_qkv, b_qkv, g_q, g_k, sinks, w_o, b_o, state_k, state_v, past_len):
    d, rb = geom.d, geom.rb
    heads = sinks.shape[0]
    hd = g_q.shape[0]
    kv = state_k.shape[-2]
    grp = heads // kv
    nq, nk = heads * hd, kv * hd
    nbuf = state_k.shape[1]
    qt = geom.q
    if WINDOW != nbuf or geom.s % WINDOW:
        raise ValueError("window buffer must hold exactly one window and tile the prompt")
    if 2 * hd != 128 or kv % 2:
        raise ValueError("the prompt kernel pairs two heads per 128-lane tile")

    head_of = jnp.arange(nk) // hd
    e = (head_of[:, None] == head_of[None, :]).astype(BF16)
    gq_t = (jnp.tile(g_q, kv) * hd ** -0.5).reshape(1, nk)
    gk_t = jnp.tile(g_k, kv).reshape(1, nk)

    q, k, v, v_t = pl.pallas_call(
        functools.partial(_swa_proj_body, nq=nq, nk=nk, hd=hd),
        grid=(geom.nb,),
        in_specs=[geom.row(d), geom.full(norm_g), geom.mod(1), geom.mod(0),
                  pl.BlockSpec(w_qkv.shape, lambda i: (0, 0)), pl.BlockSpec((1, nq + 2 * nk), lambda i: (0, 0)),
                  geom.full(gq_t), geom.full(gk_t), geom.full(e)],
        out_specs=[geom.row(nq), geom.row(nk), geom.row(nk), pl.BlockSpec((nk, rb), lambda i: (0, i))],
        out_shape=[jax.ShapeDtypeStruct((geom.t, nq), BF16), jax.ShapeDtypeStruct((geom.t, nk), F32),
                   jax.ShapeDtypeStruct((geom.t, nk), F32), jax.ShapeDtypeStruct((nk, geom.t), F32)],
        compiler_params=_params(1),
        name="swa_proj",
    )(x, norm_g, modx, modx, w_qkv.astype(BF16), b_qkv.reshape(1, -1), gq_t, gk_t, e)

    w = WINDOW
    t_idx = jnp.arange(w)[None, :]
    s_idx = jnp.arange(2 * w)[:, None]
    dist = t_idx + w - s_idx
    slopes = jnp.exp2(-ALIBI_MAX_BIAS * jnp.arange(1, heads + 1, dtype=F32) / heads)
    bias_p = jnp.where(((dist >= 0) & (dist < WINDOW))[None], -slopes[:, None, None] * dist[None].astype(F32), MASKED)
    nblk = geom.tp // w
    bpb = geom.s // w

    def prev_rows(i):
        return (jnp.where(i % bpb == 0, i, i - 1), 0)

    def prev_cols(i):
        return (0, jnp.where(i % bpb == 0, i, i - 1))

    o_p = pl.pallas_call(
        functools.partial(_swa_prompt_body, kv=kv, grp=grp, hd=hd),
        grid=(nblk,),
        in_specs=[pl.BlockSpec(memory_space=pltpu.SMEM),
                  pl.BlockSpec((w, nq), lambda i: (i, 0)),
                  pl.BlockSpec((w, nk), lambda i: (i, 0)), pl.BlockSpec((w, nk), prev_rows),
                  pl.BlockSpec((nk, w), lambda i: (0, i)), pl.BlockSpec((nk, w), prev_cols),
                  pl.BlockSpec(bias_p.shape, lambda i: (0, 0, 0))],
        out_specs=pl.BlockSpec((w, nq), lambda i: (i, 0)),
        out_shape=jax.ShapeDtypeStruct((geom.tp, nq), BF16),
        scratch_shapes=[pltpu.VMEM((nq, w), F32)],
        compiler_params=_params(1),
        name="swa_prompt_attn",
    )(sinks, q, k, k, v_t, v_t, bias_p)

    qi = jnp.arange(qt)[:, None]
    so = jnp.arange(nbuf)[None, :]
    dist_old = qi + nbuf - so
    bias_old = _alibi_bias(dist_old, (dist_old < WINDOW) & (past_len - nbuf + so >= 0), heads, kv)
    dist_new = qi - so
    bias_new = _alibi_bias(dist_new, (dist_new >= 0) & (so < qt), heads, kv)
    nbt = _pick(geom.db, (8, 4, 2, 1))
    rows = nbt * qt
    off = geom.tp // rows
    o_s, k_s, v_s = pl.pallas_call(
        functools.partial(_swa_sample_body, kv=kv, grp=grp, hd=hd, nq_tok=qt),
        grid=(geom.db // nbt,),
        in_specs=[pl.BlockSpec(memory_space=pltpu.SMEM),
                  pl.BlockSpec((rows, nq), lambda i: (i + off, 0)),
                  pl.BlockSpec((rows, nk), lambda i: (i + off, 0)), pl.BlockSpec((rows, nk), lambda i: (i + off, 0)),
                  pl.BlockSpec((nbt, nbuf, nk), lambda i: (i, 0, 0)), pl.BlockSpec((nbt, nbuf, nk), lambda i: (i, 0, 0)),
                  pl.BlockSpec(bias_old.shape, lambda i: (0, 0, 0)), pl.BlockSpec(bias_new.shape, lambda i: (0, 0, 0))],
        out_specs=[pl.BlockSpec((rows, nq), lambda i: (i, 0)),
                   pl.BlockSpec((nbt, nbuf, nk), lambda i: (i, 0, 0)), pl.BlockSpec((nbt, nbuf, nk), lambda i: (i, 0, 0))],
        out_shape=[jax.ShapeDtypeStruct((geom.ts, nq), F32),
                   jax.ShapeDtypeStruct((geom.db, nbuf, nk), F32), jax.ShapeDtypeStruct((geom.db, nbuf, nk), F32)],
        compiler_params=_params(1),
        name="swa_sample_attn",
    )(sinks, q, k, v, state_k.reshape(geom.db, nbuf, nk), state_v.reshape(geom.db, nbuf, nk), bias_old, bias_new)

    o_all = jnp.concatenate([o_p, o_s.astype(BF16)], axis=0)
    x = _out_proj(geom, o_all, w_o, b_o, x, modx, 2)

    wbuf = min(WINDOW, geom.s)
    kp = k[:geom.tp].reshape(geom.b, geom.s, kv, hd)[:, geom.s - wbuf:]
    vp = v[:geom.tp].reshape(geom.b, geom.s, kv, hd)[:, geom.s - wbuf:]
    return x, kp, vp, k_s.reshape(geom.db, nbuf, kv, hd), v_s.reshape(geom.db, nbuf, kv, hd)


def _out_proj_body(o_ref, w_ref, b_ref, x_ref, gate_ref, out_ref):
    y = _dot(o_ref[...], w_ref[...]) + b_ref[...]
    out_ref[...] = x_ref[...] + gate_ref[...] * y


def _out_proj_nobias_body(o_ref, w_ref, x_ref, gate_ref, out_ref):
    out_ref[...] = x_ref[...] + gate_ref[...] * _dot(o_ref[...], w_ref[...])


def _out_proj(geom, o_all, w_o, b_o, x, modx, gate_chunk):
    d = geom.d
    w_bf = w_o.astype(BF16)
    if b_o is None:
        body, extra, extra_specs = _out_proj_nobias_body, (), []
    else:
        body, extra, extra_specs = _out_proj_body, (b_o.reshape(1, d),), [pl.BlockSpec((1, d), lambda i: (0, 0))]
    return pl.pallas_call(
        body,
        grid=(geom.nb,),
        in_specs=[geom.row(o_all.shape[1]), geom.full(w_bf)] + extra_specs + [geom.row(d), geom.mod(gate_chunk)],
        out_specs=geom.row(d),
        out_shape=jax.ShapeDtypeStruct((geom.t, d), F32),
        compiler_params=_params(1),
        name="out_proj",
    )(o_all, w_bf, *extra, x, modx)


def _moe_prep_body(x_ref, g_ref, sc_ref, sh_ref, wr_hi_ref, wr_lo_ref, br_ref, h_ref, lg_ref):
    h = _modulate(x_ref[...], g_ref[...], sc_ref[...], sh_ref[...])
    h_hi, h_lo = _split_bf16(h)
    wr_hi = wr_hi_ref[...]
    lg_ref[...] = _dot(h_hi, wr_hi) + _dot(h_hi, wr_lo_ref[...]) + _dot(h_lo, wr_hi) + br_ref[...]
    h_ref[...] = h


def _row_copy(src_hbm, src_row, dst_vmem, dst_row, sem):
    return pltpu.make_async_copy(src_hbm.at[pl.ds(src_row, 1), :], dst_vmem.at[pl.ds(dst_row, 1), :], sem)


def _gather_rows_start(idx_ref, base, n_rows, src_hbm, dst_vmem, sem):
    def body(g, carry):
        for k in range(GATHER_UNROLL):
            r = g * GATHER_UNROLL + k
            _row_copy(src_hbm, idx_ref[base + r], dst_vmem, r, sem).start()
        return carry

    lax.fori_loop(0, n_rows // GATHER_UNROLL, body, 0)


def _gather_rows_wait(n_rows, src_hbm, dst_vmem, sem):
    def body(g, carry):
        for k in range(GATHER_UNROLL):
            _row_copy(src_hbm, 0, dst_vmem, 0, sem).wait()
        return carry

    lax.fori_loop(0, n_rows // GATHER_UNROLL, body, 0)


def _moe_expert_body(be_ref, nvalid_ref, tok_ref, h_hbm, wg_ref, wu_ref, bg_ref, bu_ref, wd_ref, bd_ref, gate_ref,
                     out_ref, xbuf, xbf, sem):
    blk, n = pl.program_id(0), pl.program_id(1)
    last = pl.num_programs(1) - 1
    n_valid = nvalid_ref[0]
    valid = blk < n_valid
    tm = xbf.shape[0]

    @pl.when(n == 0)
    def _():
        @pl.when(blk == 0)
        def _():
            _gather_rows_start(tok_ref, 0, tm, h_hbm, xbuf, sem.at[0])

        @pl.when(valid)
        def _():
            _gather_rows_wait(tm, h_hbm, xbuf, sem.at[0])
            xbf[...] = xbuf[...].astype(BF16)
            out_ref[...] = jnp.broadcast_to(bd_ref[...], out_ref.shape)

        @pl.when(blk + 1 < n_valid)
        def _():
            _gather_rows_start(tok_ref, (blk + 1) * tm, tm, h_hbm, xbuf, sem.at[0])

    @pl.when(valid)
    def _():
        x = xbf[...]
        gl = jnp.minimum(_dot(x, wg_ref[...].astype(BF16)) + bg_ref[...], SWIGLU_LIMIT)
        up = jnp.clip(_dot(x, wu_ref[...].astype(BF16)) + bu_ref[...], -SWIGLU_LIMIT, SWIGLU_LIMIT)
        act = (up + 1.0) * gl * jax.nn.sigmoid(SWIGLU_ALPHA * gl)
        out_ref[...] += _dot(act.astype(BF16), wd_ref[...].astype(BF16))

        @pl.when(n == last)
        def _():
            out_ref[...] *= gate_ref[...]

    @pl.when(jnp.logical_and(jnp.logical_not(valid), n == 0))
    def _():
        out_ref[...] = jnp.zeros_like(out_ref)


def _moe_layer(geom, layer, x, modx, norm_g, w_r, b_r, w_gu, b_gu, w_d, b_d):
    d, t = geom.d, geom.t
    n_exp = w_r.shape[-1]
    de = w_d.shape[2]
    lanes = 128
    e_pad = -(-n_exp // lanes) * lanes
    wr_hi, wr_lo = _split_bf16(jnp.pad(w_r, ((0, 0), (0, e_pad - n_exp))))
    br = jnp.pad(b_r, (0, e_pad - n_exp)).reshape(1, e_pad)

    h, logits = pl.pallas_call(
        _moe_prep_body,
        grid=(geom.nb,),
        in_specs=[geom.row(d), geom.full(norm_g), geom.mod(4), geom.mod(3),
                  geom.full(wr_hi), geom.full(wr_lo), geom.full(br)],
        out_specs=[geom.row(d), geom.row(e_pad)],
        out_shape=[jax.ShapeDtypeStruct((t, d), F32), jax.ShapeDtypeStruct((t, e_pad), F32)],
        compiler_params=_params(1),
        name="moe_prep",
    )(x, norm_g, modx, modx, wr_hi, wr_lo, br)

    tm = EXPERT_ROWS
    top_v, top_e = lax.top_k(logits[:, :n_exp], TOP_K)
    gates = jax.nn.softmax(top_v, axis=-1).reshape(-1)
    e_flat = top_e.reshape(-1).astype(jnp.int32)
    n_assign = t * TOP_K
    order = jnp.argsort(e_flat).astype(jnp.int32)
    rank = jnp.argsort(order).astype(jnp.int32)
    bounds = jnp.searchsorted(e_flat[order], jnp.arange(n_exp + 1, dtype=jnp.int32), side='left').astype(jnp.int32)
    cum, counts = bounds[:-1], bounds[1:] - bounds[:-1]
    padded = (counts + tm - 1) // tm * tm
    pad_end = jnp.cumsum(padded)
    start = pad_end - padded
    n_blocks = -(-(n_assign + n_exp * (tm - 1)) // tm)
    n_slots = n_blocks * tm
    blk_e = jnp.minimum(jnp.searchsorted(pad_end, jnp.arange(n_blocks, dtype=jnp.int32) * tm, side='right'),
                        n_exp - 1).astype(jnp.int32)
    n_valid = (pad_end[-1] // tm).astype(jnp.int32).reshape(1)
    blk_i = jnp.arange(n_blocks, dtype=jnp.int32)
    r_s = (blk_i * tm - start[blk_e])[:, None] + jnp.arange(tm, dtype=jnp.int32)[None, :]
    live = (r_s < counts[blk_e][:, None]) & (blk_i < n_valid[0])[:, None]
    a_s = order[jnp.clip(cum[blk_e][:, None] + r_s, 0, n_assign - 1)]
    tok = jnp.where(live, a_s // TOP_K, 0).astype(jnp.int32).reshape(-1)
    gate = jnp.where(live, gates[a_s], 0.0).reshape(-1)
    slot_of = (start[e_flat] + rank - cum[e_flat]).astype(jnp.int32)

    tn = _pick(de, (EXPERT_COLS, 256, 128))
    nt = de // tn
    out = pl.pallas_call(
        _moe_expert_body,
        grid_spec=pltpu.PrefetchScalarGridSpec(
            num_scalar_prefetch=3,
            grid=(n_blocks, nt),
            in_specs=[pl.BlockSpec(memory_space=pl.ANY),
                      pl.BlockSpec((None, None, d, tn), lambda b, n, be, nv, tk: (layer, be[b], 0, n)),
                      pl.BlockSpec((None, None, d, tn), lambda b, n, be, nv, tk: (layer, be[b], 0, nt + n)),
                      pl.BlockSpec((None, None, 1, tn), lambda b, n, be, nv, tk: (layer, be[b], 0, n)),
                      pl.BlockSpec((None, None, 1, tn), lambda b, n, be, nv, tk: (layer, be[b], 0, nt + n)),
                      pl.BlockSpec((None, None, tn, d), lambda b, n, be, nv, tk: (layer, be[b], n, 0)),
                      pl.BlockSpec((None, None, 1, d), lambda b, n, be, nv, tk: (layer, be[b], 0, 0)),
                      pl.BlockSpec((tm, 1), lambda b, n, be, nv, tk: (b, 0))],
            out_specs=pl.BlockSpec((tm, d), lambda b, n, be, nv, tk: (b, 0)),
            scratch_shapes=[pltpu.VMEM((tm, d), F32), pltpu.VMEM((tm, d), BF16), pltpu.SemaphoreType.DMA((1,))]),
        out_shape=jax.ShapeDtypeStruct((n_slots, d), F32),
        compiler_params=_params(2, gathers=True),
        name="moe_experts",
    )(blk_e, n_valid, tok, h, w_gu, w_gu, b_gu.reshape(*b_gu.shape[:2], 1, -1), b_gu.reshape(*b_gu.shape[:2], 1, -1),
      w_d, b_d.reshape(*b_d.shape[:2], 1, -1), gate.reshape(n_slots, 1))

    return _moe_combine(geom, x, out, slot_of, modx, 5)


def _moe_combine_body(slot_ref, out_hbm, x_ref, gate_ref, o_ref, ybuf, sem, *, top_k):
    i = pl.program_id(0)
    rb = x_ref.shape[0]
    n_rows = top_k * rb
    slot = i % 2

    @pl.when(i == 0)
    def _():
        _gather_rows_start(slot_ref, 0, n_rows, out_hbm, ybuf.at[0], sem.at[0])

    @pl.when(i + 1 < pl.num_programs(0))
    def _():
        _gather_rows_start(slot_ref, (i + 1) * n_rows, n_rows, out_hbm, ybuf.at[1 - slot], sem.at[1 - slot])

    _gather_rows_wait(n_rows, out_hbm, ybuf.at[slot], sem.at[slot])
    y = ybuf[slot, 0:rb, :]
    for k in range(1, top_k):
        y = y + ybuf[slot, k * rb:(k + 1) * rb, :]
    o_ref[...] = x_ref[...] + gate_ref[...] * y


def _moe_combine(geom, x, out, slot_of, modx, gate_chunk):
    d = geom.d
    return pl.pallas_call(
        functools.partial(_moe_combine_body, top_k=TOP_K),
        grid_spec=pltpu.PrefetchScalarGridSpec(
            num_scalar_prefetch=1,
            grid=(geom.nb,),
            in_specs=[pl.BlockSpec(memory_space=pl.ANY),
                      pl.BlockSpec((geom.rb, d), lambda i, sl: (i, 0)),
                      geom.mod(gate_chunk)],
            out_specs=pl.BlockSpec((geom.rb, d), lambda i, sl: (i, 0)),
            scratch_shapes=[pltpu.VMEM((2, TOP_K * geom.rb, d), F32), pltpu.SemaphoreType.DMA((2,))]),
        out_shape=jax.ShapeDtypeStruct((geom.t, d), F32),
        compiler_params=_params(1, gathers=True),
        name="moe_combine",
    )(slot_of.reshape(geom.nb, geom.rb, TOP_K).transpose(0, 2, 1).reshape(-1), out, x, modx)


def _rope_tables(pos, half):
    inv = ROPE_THETA ** (-jnp.arange(half, dtype=F32) / half)
    ang = pos.astype(F32)[:, None] * inv
    return jnp.cos(ang), jnp.sin(ang)


def _mla_down_body(x_ref, g_ref, sc_ref, sh_ref, w_ref, gqa_ref, gkva_ref,
                   cq_ref, ckv_ref, kr_ref, krs_ref, *, ql, kl, rope):
    h = _modulate(x_ref[...], g_ref[...], sc_ref[...], sh_ref[...]).astype(BF16)
    dd = _dot(h, w_ref[...])
    dq = dd[:, :ql]
    cq_ref[...] = (dq * lax.rsqrt(jnp.mean(dq * dq, axis=-1, keepdims=True) + NORM_EPS) * gqa_ref[...]).astype(BF16)
    dk = dd[:, ql:ql + kl]
    ckv_ref[...] = dk * lax.rsqrt(jnp.mean(dk * dk, axis=-1, keepdims=True) + NORM_EPS) * gkva_ref[...]
    kr_ref[...] = dd[:, ql + kl:ql + kl + rope]
    krs_ref[...] = dd[:, ql + kl + rope:]


def _mla_q_prompt_body(cq_ref, w_ref, a_ref, b_ref, q_ref, *, heads, hw, qk):
    cq = cq_ref[...]
    a, bt = a_ref[...], b_ref[...]
    for h in range(heads):
        x = _dot(cq, w_ref[:, h * hw:(h + 1) * hw])
        xs = _dot(cq, w_ref[:, (heads + h) * hw:(heads + h + 1) * hw])
        r = lax.rsqrt(jnp.sum(x * x, axis=-1, keepdims=True) * (1.0 / qk) + NORM_EPS)
        q_ref[:, h * hw:(h + 1) * hw] = ((x * a + xs * bt) * r).astype(BF16)


def _mla_q_sample_body(cq_ref, w_ref, a_ref, b_ref, a2_ref, b2_ref, wuk_ref, qabs_ref, qr_ref,
                       *, heads, hw, qk, nope, rope, nq_tok):
    cq = cq_ref[...]
    rb = cq.shape[0]
    nb = rb // nq_tok
    a, bt = a_ref[...], b_ref[...]
    a2, b2 = a2_ref[...], b2_ref[...]
    for h in range(heads):
        x = _dot(cq, w_ref[:, h * hw:(h + 1) * hw])
        xs = _dot(cq, w_ref[:, (heads + h) * hw:(heads + h + 1) * hw])
        r = lax.rsqrt(jnp.sum(x * x, axis=-1, keepdims=True) * (1.0 / qk) + NORM_EPS)
        qn = (x[:, :nope] * a[:, :nope] * r).astype(BF16)
        qabs = _dot_nt(qn, wuk_ref[:, h * nope:(h + 1) * nope])
        xr, xsr = x[:, nope:nope + rope], xs[:, nope:nope + rope]
        q1 = (xr * a[:, nope:nope + rope] + xsr * bt[:, nope:nope + rope]) * r
        q2 = (xsr * a2 + xr * b2) * r
        rows = slice(h * nq_tok, (h + 1) * nq_tok)
        qabs_ref[:, rows, :] = qabs.reshape(nb, nq_tok, qabs.shape[1])
        qr_ref[:, rows, :rope] = q1.reshape(nb, nq_tok, rope)
        qr_ref[:, rows, rope:] = q2.reshape(nb, nq_tok, rope)


def _mla_kv_prompt_body(ckv_ref, kr_ref, krs_ref, wuk_ref, wuv_ref, gkn_ref, ak_ref, bk_ref, k_ref, v_ref,
                        *, heads, hw, qk, nope, rope, vd):
    c = ckv_ref[...].astype(BF16)
    kr = kr_ref[...]
    ssr = jnp.sum(kr * kr, axis=-1, keepdims=True)
    krot = kr * ak_ref[...] + krs_ref[...] * bk_ref[...]
    gkn = gkn_ref[...]
    zeros = jnp.zeros((c.shape[0], hw - nope - rope), BF16)
    for h in range(heads):
        kn = _dot(c, wuk_ref[:, h * nope:(h + 1) * nope])
        r = lax.rsqrt((jnp.sum(kn * kn, axis=-1, keepdims=True) + ssr) * (1.0 / qk) + NORM_EPS)
        k_ref[:, h * hw:h * hw + nope] = (kn * gkn * r).astype(BF16)
        k_ref[:, h * hw + nope:h * hw + nope + rope] = (krot * r).astype(BF16)
        k_ref[:, h * hw + nope + rope:(h + 1) * hw] = zeros
        v_ref[:, h * vd:(h + 1) * vd] = _dot(c, wuv_ref[:, h * vd:(h + 1) * vd]).astype(BF16)


def _flash_body(qi_ref, kj_ref, q_ref, k_ref, v_ref, o_ref, m_ref, l_ref, acc_ref):
    n = pl.program_id(1)
    qi, kj = qi_ref[n], kj_ref[n]
    tq, tk = q_ref.shape[0], k_ref.shape[0]

    @pl.when(kj == 0)
    def _():
        m_ref[...] = jnp.full_like(m_ref, MASKED)
        l_ref[...] = jnp.zeros_like(l_ref)
        acc_ref[...] = jnp.zeros_like(acc_ref)

    k = k_ref[...]
    v = v_ref[...]
    cols = kj * tk + lax.broadcasted_iota(jnp.int32, (1, tk), 1)
    sub = min(tq, FLASH_SUB_ROWS)
    for r0 in range(0, tq, sub):
        rs = slice(r0, r0 + sub)
        s = _dot_nt(q_ref[rs, :], k)
        rows = qi * tq + r0 + lax.broadcasted_iota(jnp.int32, (sub, 1), 0)
        s = jnp.where(cols <= rows, s, MASKED)
        m_old = m_ref[rs, :]
        m_new = jnp.maximum(m_old, jnp.max(s, axis=-1, keepdims=True))
        p = jnp.exp(s - m_new)
        alpha = jnp.exp(m_old - m_new)
        l_ref[rs, :] = l_ref[rs, :] * alpha + jnp.sum(p, axis=-1, keepdims=True)
        acc_ref[rs, :] = acc_ref[rs, :] * alpha + _dot(p.astype(BF16), v)
        m_ref[rs, :] = m_new

    @pl.when(kj == qi)
    def _():
        o_ref[...] = (acc_ref[...] * (1.0 / l_ref[...])).astype(BF16)


def _mla_prompt_attn(q_p, k_p, v_p, seq_index, s, heads, hw, vd):
    tq = _pick(s, (FLASH_ROWS, 512, 256, 128, 64))
    nqb = s // tq
    pairs = [(i, j) for i in range(nqb) for j in range(i + 1)]
    qi_tab = jnp.asarray([p[0] for p in pairs], jnp.int32)
    kj_tab = jnp.asarray([p[1] for p in pairs], jnp.int32)
    off = seq_index * nqb
    return pl.pallas_call(
        _flash_body,
        grid_spec=pltpu.PrefetchScalarGridSpec(
            num_scalar_prefetch=2,
            grid=(heads, len(pairs)),
            in_specs=[pl.BlockSpec((tq, hw), lambda h, n, qi, kj: (off + qi[n], h)),
                      pl.BlockSpec((tq, hw), lambda h, n, qi, kj: (off + kj[n], h)),
                      pl.BlockSpec((tq, vd), lambda h, n, qi, kj: (off + kj[n], h))],
            out_specs=pl.BlockSpec((tq, vd), lambda h, n, qi, kj: (qi[n], h)),
            scratch_shapes=[pltpu.VMEM((tq, 1), F32), pltpu.VMEM((tq, 1), F32), pltpu.VMEM((tq, vd), F32)]),
        out_shape=jax.ShapeDtypeStruct((s, heads * vd), BF16),
        compiler_params=_params(2),
        name="mla_prompt_attn",
    )(qi_tab, kj_tab, q_p, k_p, v_p)


def _mla_sample_body(pt_ref, qabs_ref, qr_ref, cnew_ref, krnew_ref, cs_ref, csn_ref, wukt_ref, ckv_hbm, krt_hbm,
                     out_ref, cbuf, krbuf, sem_c, sem_k, qabs_bf, qr_bf, m_ref, l_ref, acc_ref,
                     *, layer, heads, nq_tok, nope, qk, pages, chunk_pages):
    b, j = pl.program_id(0), pl.program_id(1)
    nj = pl.num_programs(1)
    step = b * nj + j
    last_step = pl.num_programs(0) * nj - 1
    slot = step % 2
    hq = heads * nq_tok
    page = cbuf.shape[2]

    def page_copies(step_, slot_, u):
        phys = pt_ref[step_ * pages + u]
        return (pltpu.make_async_copy(ckv_hbm.at[layer, phys], cbuf.at[slot_, u], sem_c.at[slot_]),
                pltpu.make_async_copy(krt_hbm.at[layer, phys], krbuf.at[slot_, u], sem_k.at[slot_]))

    def start_pages(step_, slot_, us):
        for u in us:
            for cp in page_copies(step_, slot_, u):
                cp.start()

    def wait_pages(step_, slot_):
        for u in range(pages):
            for cp in page_copies(step_, slot_, u):
                cp.wait()

    @pl.when(step == 0)
    def _():
        start_pages(0, 0, range(pages))

    wait_pages(step, slot)
    next_step = jnp.minimum(step + 1, last_step)

    @pl.when(j == 0)
    def _():
        m_ref[...] = jnp.full_like(m_ref, MASKED)
        l_ref[...] = jnp.zeros_like(l_ref)
        acc_ref[...] = jnp.zeros_like(acc_ref)
        qabs_bf[...] = qabs_ref[...].astype(BF16)
        qr_bf[...] = qr_ref[...].astype(BF16)

    def scores(c_bf, kr_t, cs):
        n = c_bf.shape[0]
        ssn_parts = []
        for h0 in range(0, heads, SAMPLE_HEAD_GROUP):
            kt = _dot_nt(wukt_ref[h0 * nope:(h0 + SAMPLE_HEAD_GROUP) * nope, :], c_bf)
            ssn_parts.append(jnp.sum((kt * kt).reshape(SAMPLE_HEAD_GROUP, nope, n), axis=1))
        ssn = jnp.concatenate(ssn_parts, axis=0)
        ssr = jnp.sum(kr_t * kr_t, axis=0, keepdims=True)
        rk = lax.rsqrt((ssn + ssr) * (1.0 / qk) + NORM_EPS)
        k_rot = (jnp.concatenate([kr_t, kr_t], axis=0) * cs).astype(BF16)
        s = _dot_nt(qabs_bf[...], c_bf) + _dot(qr_bf[...], k_rot)
        return s * jnp.broadcast_to(rk[:, None, :], (heads, nq_tok, n)).reshape(hq, n)

    def update(s, c_bf):
        m_old = m_ref[...]
        m_new = jnp.maximum(m_old, jnp.max(s, axis=-1, keepdims=True))
        p = jnp.exp(s - m_new)
        alpha = jnp.exp(m_old - m_new)
        l_ref[...] = l_ref[...] * alpha + jnp.sum(p, axis=-1, keepdims=True)
        acc_ref[...] = acc_ref[...] * alpha + _dot(p.astype(BF16), c_bf)
        m_ref[...] = m_new

    s_parts, c_parts = [], []
    for u in range(0, pages, chunk_pages):
        us = range(u, u + chunk_pages)
        c_bf = jnp.concatenate([cbuf[slot, v].astype(BF16) for v in us], axis=0)
        kr_t = jnp.concatenate([krbuf[slot, v] for v in us], axis=1)
        s_parts.append(scores(c_bf, kr_t, cs_ref[:, u * page:(u + chunk_pages) * page]))
        c_parts.append(c_bf)
        start_pages(next_step, 1 - slot, us)
    update(jnp.concatenate(s_parts, axis=1), jnp.concatenate(c_parts, axis=0))

    @pl.when(j == nj - 1)
    def _():
        n_new = cnew_ref.shape[0]
        row_q = lax.broadcasted_iota(jnp.int32, (hq, n_new), 0) % nq_tok
        col = lax.broadcasted_iota(jnp.int32, (hq, n_new), 1)
        c_bf = cnew_ref[...].astype(BF16)
        s = scores(c_bf, krnew_ref[...], csn_ref[...])
        update(jnp.where(col <= row_q, s, MASKED), c_bf)
        out_ref[...] = acc_ref[...] * (1.0 / l_ref[...])

    @pl.when(step == last_step)
    def _():
        wait_pages(last_step, 1 - slot)


def _mla_sample_attn(layer, page_table, qabs, qr, cnew, krnew_t, cache_ckv, cache_kr_t, cs_old, cs_new, w_uk_t,
                     *, heads, nq_tok, nope, qk):
    db, hq, kl = qabs.shape
    n_pages = page_table.shape[1]
    page = cache_ckv.shape[2]
    rope = cache_kr_t.shape[2]
    n_new = cnew.shape[1]
    pg = _pick(n_pages, (PAGES_PER_STEP, 8, 4, 2, 1))
    chunk_pages = _pick(pg, (SAMPLE_CHUNK_PAGES, 2, 1))
    return pl.pallas_call(
        functools.partial(_mla_sample_body, layer=layer, heads=heads, nq_tok=nq_tok, nope=nope, qk=qk, pages=pg,
                          chunk_pages=chunk_pages),
        grid_spec=pltpu.PrefetchScalarGridSpec(
            num_scalar_prefetch=1,
            grid=(db, n_pages // pg),
            in_specs=[pl.BlockSpec((None, hq, kl), lambda b, j, pt: (b, 0, 0)),
                      pl.BlockSpec((None, hq, 2 * rope), lambda b, j, pt: (b, 0, 0)),
                      pl.BlockSpec((None, n_new, kl), lambda b, j, pt: (b, 0, 0)),
                      pl.BlockSpec((None, rope, n_new), lambda b, j, pt: (b, 0, 0)),
                      pl.BlockSpec((2 * rope, pg * page), lambda b, j, pt: (0, j)),
                      pl.BlockSpec((2 * rope, n_new), lambda b, j, pt: (0, 0)),
                      pl.BlockSpec(w_uk_t.shape, lambda b, j, pt: (0, 0)),
                      pl.BlockSpec(memory_space=pl.ANY),
                      pl.BlockSpec(memory_space=pl.ANY)],
            out_specs=pl.BlockSpec((None, hq, kl), lambda b, j, pt: (b, 0, 0)),
            scratch_shapes=[pltpu.VMEM((2, pg, page, kl), F32), pltpu.VMEM((2, pg, rope, page), F32),
                            pltpu.SemaphoreType.DMA((2,)), pltpu.SemaphoreType.DMA((2,)),
                            pltpu.VMEM((hq, kl), BF16), pltpu.VMEM((hq, 2 * rope), BF16),
                            pltpu.VMEM((hq, 1), F32), pltpu.VMEM((hq, 1), F32), pltpu.VMEM((hq, kl), F32)]),
        out_shape=jax.ShapeDtypeStruct((db, hq, kl), F32),
        compiler_params=_params(2, gathers=True),
        name="mla_sample_attn",
    )(page_table.reshape(-1), qabs, qr, cnew, krnew_t, cs_old, cs_new, w_uk_t, cache_ckv, cache_kr_t)


def _mla_uv_body(lat_ref, wuv_ref, o_ref, *, heads, nq_tok, vd):
    nb, _, kl = lat_ref.shape
    for h in range(heads):
        lat = lat_ref[:, h * nq_tok:(h + 1) * nq_tok, :].reshape(nb * nq_tok, kl).astype(BF16)
        o_ref[:, h * vd:(h + 1) * vd] = _dot(lat, wuv_ref[:, h * vd:(h + 1) * vd])


def _mla_layer(geom, layer, x, modx, norm_g, w_dkv, g_qa, g_kva, w_uq, g_q, w_uk, g_k, w_uv, w_o,
               cache_ckv, cache_kr, page_table, past_len):
    d, rb, qt = geom.d, geom.rb, geom.q
    ql, kl = g_qa.shape[0], g_kva.shape[0]
    rope = cache_kr.shape[-1]
    half = rope // 2
    qk = g_q.shape[0]
    nope = qk - rope
    heads = w_uq.shape[1] // qk
    vd = w_uv.shape[1] // heads
    hw = MXU_WIDTH
    scale = qk ** -0.5
    page = cache_ckv.shape[2]
    n_pages = page_table.shape[1]
    tp, ts = geom.tp, geom.ts

    w_dn = jnp.concatenate([w_dkv, w_dkv[:, ql + kl + half:], w_dkv[:, ql + kl:ql + kl + half]], axis=1).astype(BF16)
    gqa, gkva = g_qa.reshape(1, ql), g_kva.reshape(1, kl)
    cq, ckv, kr, krs = pl.pallas_call(
        functools.partial(_mla_down_body, ql=ql, kl=kl, rope=rope),
        grid=(geom.nb,),
        in_specs=[geom.row(d), geom.full(norm_g), geom.mod(1), geom.mod(0), geom.full(w_dn),
                  geom.full(gqa), geom.full(gkva)],
        out_specs=[geom.row(ql), geom.row(kl), geom.row(rope), geom.row(rope)],
        out_shape=[jax.ShapeDtypeStruct((geom.t, ql), BF16), jax.ShapeDtypeStruct((geom.t, kl), F32),
                   jax.ShapeDtypeStruct((geom.t, rope), F32), jax.ShapeDtypeStruct((geom.t, rope), F32)],
        compiler_params=_params(1),
        name="mla_down",
    )(x, norm_g, modx, modx, w_dn, gqa, gkva)

    w3 = w_uq.reshape(ql, heads, qk)
    wn, w1, w2 = w3[:, :, :nope], w3[:, :, nope:nope + half], w3[:, :, nope + half:]
    zpad = jnp.zeros((ql, heads, hw - qk), F32)
    w_q = jnp.concatenate([jnp.concatenate([wn, w1, w2, zpad], axis=-1).reshape(ql, heads * hw),
                           jnp.concatenate([jnp.zeros_like(wn), w2, w1, zpad], axis=-1).reshape(ql, heads * hw)],
                          axis=1).astype(BF16)
    gqn, gq1, gq2 = g_q[:nope], g_q[nope:nope + half], g_q[nope + half:]
    gkn, gk1, gk2 = g_k[:nope], g_k[nope:nope + half], g_k[nope + half:]

    def q_tables(pos, nope_gain):
        cos, sin = _rope_tables(pos, half)
        n = pos.shape[0]
        z = jnp.zeros((n, hw - qk), F32)
        a = jnp.concatenate([jnp.broadcast_to(nope_gain, (n, nope)), gq1 * cos, gq2 * cos, z], axis=1) * scale
        b = jnp.concatenate([jnp.zeros((n, nope), F32), -gq2 * sin, gq1 * sin, z], axis=1) * scale
        return a, b, cos, sin

    pos_p = jnp.tile(jnp.arange(geom.s, dtype=jnp.int32), geom.b)
    pos_s = jnp.tile(past_len + jnp.arange(qt, dtype=jnp.int32), geom.db)
    a_p, b_p, cos_p, sin_p = q_tables(pos_p, gqn)
    a_s, b_s, cos_s, sin_s = q_tables(pos_s, gqn * gkn)
    a2_s = jnp.concatenate([gq2 * cos_s, -gq1 * cos_s], axis=1) * scale
    b2_s = jnp.concatenate([gq1 * sin_s, gq2 * sin_s], axis=1) * scale

    q_p = pl.pallas_call(
        functools.partial(_mla_q_prompt_body, heads=heads, hw=hw, qk=qk),
        grid=(geom.nbp,),
        in_specs=[geom.row(ql), geom.full(w_q), geom.row(hw), geom.row(hw)],
        out_specs=geom.row(heads * hw),
        out_shape=jax.ShapeDtypeStruct((tp, heads * hw), BF16),
        compiler_params=_params(1),
        name="mla_q_prompt",
    )(cq, w_q, a_p, b_p)

    w_uk_bf = w_uk.astype(BF16)
    nbq = rb // qt
    hq = heads * qt
    qabs, qr = pl.pallas_call(
        functools.partial(_mla_q_sample_body, heads=heads, hw=hw, qk=qk, nope=nope, rope=rope, nq_tok=qt),
        grid=(geom.nbs,),
        in_specs=[geom.row(ql, geom.nbp), geom.full(w_q), geom.row(hw), geom.row(hw), geom.row(rope), geom.row(rope),
                  geom.full(w_uk_bf)],
        out_specs=[pl.BlockSpec((nbq, hq, kl), lambda i: (i, 0, 0)),
                   pl.BlockSpec((nbq, hq, 2 * rope), lambda i: (i, 0, 0))],
        out_shape=[jax.ShapeDtypeStruct((geom.db, hq, kl), F32), jax.ShapeDtypeStruct((geom.db, hq, 2 * rope), F32)],
        compiler_params=_params(1),
        name="mla_q_sample",
    )(cq, w_q, a_s, b_s, a2_s, b2_s, w_uk_bf)

    w_uv_bf = w_uv.astype(BF16)
    a_k = jnp.concatenate([gk1 * cos_p, gk2 * cos_p], axis=1)
    b_k = jnp.concatenate([-gk2 * sin_p, gk1 * sin_p], axis=1)
    gkn_row = gkn.reshape(1, nope)
    k_p, v_p = pl.pallas_call(
        functools.partial(_mla_kv_prompt_body, heads=heads, hw=hw, qk=qk, nope=nope, rope=rope, vd=vd),
        grid=(geom.nbp,),
        in_specs=[geom.row(kl), geom.row(rope), geom.row(rope), geom.full(w_uk_bf), geom.full(w_uv_bf),
                  geom.full(gkn_row), geom.row(rope), geom.row(rope)],
        out_specs=[geom.row(heads * hw), geom.row(heads * vd)],
        out_shape=[jax.ShapeDtypeStruct((tp, heads * hw), BF16), jax.ShapeDtypeStruct((tp, heads * vd), BF16)],
        compiler_params=_params(1),
        name="mla_kv_prompt",
    )(ckv, kr, krs, w_uk_bf, w_uv_bf, gkn_row, a_k, b_k)

    o_parts = [_mla_prompt_attn(q_p, k_p, v_p, bi, geom.s, heads, hw, vd) for bi in range(geom.b)]

    n_new = 128
    ckv_s = ckv[tp:].reshape(geom.db, qt, kl)
    kr_s = kr[tp:].reshape(geom.db, qt, rope)
    cnew = jnp.pad(ckv_s, ((0, 0), (0, n_new - qt), (0, 0)))
    krnew_t = jnp.pad(kr_s, ((0, 0), (0, n_new - qt), (0, 0))).transpose(0, 2, 1)
    cos_c, sin_c = _rope_tables(jnp.arange(past_len + n_new, dtype=jnp.int32), half)
    cs = jnp.concatenate([gk1 * cos_c, gk2 * cos_c, gk1 * sin_c, gk2 * sin_c], axis=1).T
    o_lat = _mla_sample_attn(layer, page_table, qabs, qr, cnew, krnew_t, cache_ckv, cache_kr.transpose(0, 1, 3, 2),
                             cs[:, :past_len], cs[:, past_len:], w_uk.T.astype(BF16),
                             heads=heads, nq_tok=qt, nope=nope, qk=qk)

    o_s = pl.pallas_call(
        functools.partial(_mla_uv_body, heads=heads, nq_tok=qt, vd=vd),
        grid=(geom.nbs,),
        in_specs=[pl.BlockSpec((nbq, hq, kl), lambda i: (i, 0, 0)), geom.full(w_uv_bf)],
        out_specs=geom.row(heads * vd),
        out_shape=jax.ShapeDtypeStruct((ts, heads * vd), F32),
        compiler_params=_params(1),
        name="mla_uv",
    )(o_lat, w_uv_bf)

    o_all = jnp.concatenate(o_parts + [o_s.astype(BF16)], axis=0)
    x = _out_proj(geom, o_all, w_o, None, x, modx, 2)
    return (x, ckv[:tp].reshape(geom.b, geom.s, kl), kr[:tp].reshape(geom.b, geom.s, rope), ckv_s, kr_s)


def kernel(x_prompt, x_sample, state_swa_k, state_swa_v, cache_mla_ckv, cache_mla_kr, page_table, c_prompt, c_sample, ada_w, ada_b, norm_attn_g, norm_ffn_g, swa_w_qkv, swa_b_qkv, swa_g_q, swa_g_k, swa_sinks, swa_w_o, swa_b_o, mla_w_dkv, mla_g_qa, mla_g_kva, mla_w_uq, mla_g_q, mla_w_uk, mla_g_k, mla_w_uv, mla_w_o, moe_w_router, moe_b_router, moe_w_gate_up, moe_b_gate_up, moe_w_down, moe_b_down):
    b, s, d = x_prompt.shape
    db, qt, _ = x_sample.shape
    depth = ada_w.shape[0]
    past_len = page_table.shape[1] * cache_mla_ckv.shape[2]
    geom = _Geom(b, s, db, qt, d)

    n_c = b + db
    m_pad = -(-n_c // 8) * 8
    c_all = jnp.pad(jnp.concatenate([c_prompt, c_sample], axis=0), ((0, m_pad - n_c), (0, 0)))
    mod = _adaln(c_all, ada_w, ada_b)

    x = jnp.concatenate([x_prompt.reshape(b * s, d), x_sample.reshape(db * qt, d)], axis=0)
    swa_kp, swa_vp, swa_ks, swa_vs = [], [], [], []
    ckv_p, kr_p, ckv_s, kr_s = [], [], [], []
    n_mixers = 2
    for i in range(depth):
        modx = geom.expand_mod(mod[i])
        g_attn = norm_attn_g[i].reshape(1, d)
        g_ffn = norm_ffn_g[i].reshape(1, d)
        j = i // n_mixers
        if i % n_mixers == 0:
            x, kp, vp, ks, vs = _swa_layer(geom, x, modx, g_attn, swa_w_qkv[j], swa_b_qkv[j], swa_g_q[j], swa_g_k[j],
                                           swa_sinks[j], swa_w_o[j], swa_b_o[j], state_swa_k[j], state_swa_v[j], past_len)
            swa_kp.append(kp)
            swa_vp.append(vp)
            swa_ks.append(ks)
            swa_vs.append(vs)
        else:
            x, cp, rp, cs, rs = _mla_layer(geom, j, x, modx, g_attn, mla_w_dkv[j], mla_g_qa[j], mla_g_kva[j], mla_w_uq[j],
                                           mla_g_q[j], mla_w_uk[j], mla_g_k[j], mla_w_uv[j], mla_w_o[j],
                                           cache_mla_ckv, cache_mla_kr, page_table, past_len)
            ckv_p.append(cp)
            kr_p.append(rp)
            ckv_s.append(cs)
            kr_s.append(rs)
        x = _moe_layer(geom, i, x, modx, g_ffn, moe_w_router[i], moe_b_router[i], moe_w_gate_up, moe_b_gate_up,
                       moe_w_down, moe_b_down)

    y_prompt = x[:b * s].reshape(b, s, d)
    y_sample = x[b * s:].reshape(db, qt, d)
    return (y_prompt, y_sample, jnp.stack(swa_kp), jnp.stack(swa_vp), jnp.stack(swa_ks), jnp.stack(swa_vs),
            jnp.stack(ckv_p), jnp.stack(kr_p), jnp.stack(ckv_s), jnp.stack(kr_s))
```

```python
import functools

import jax
import jax.numpy as jnp
from jax import lax
from jax.experimental import pallas as pl
from jax.experimental.pallas import tpu as pltpu

F32 = jnp.float32
BF16 = jnp.bfloat16

WINDOW = 128
ALIBI_MAX_BIAS = 8.0
ROPE_THETA = 10000.0
TOP_K = 4
SWIGLU_LIMIT = 7.0
SWIGLU_ALPHA = 1.702
NORM_EPS = 1e-6

MASKED = -1e30
VMEM_LIMIT_BYTES = 56 * 1024 * 1024
MXU_WIDTH = 256
EXPERT_ROWS = 512
EXPERT_COLS = 512
GATHER_UNROLL = 8
PAGES_PER_STEP = 32
FLASH_ROWS = 2048
FLASH_SUB_ROWS = 256
SAMPLE_CHUNK_PAGES = 2
SAMPLE_HEAD_GROUP = 16


def _pick(n, prefs):
    for p in prefs:
        if n % p == 0:
            return p
    raise ValueError(f"no tile in {prefs} divides {n}")


def _dot(a, b):
    return jnp.dot(a, b, preferred_element_type=F32)


def _dot_nt(a, b):
    return lax.dot_general(a, b, (((1,), (1,)), ((), ())), preferred_element_type=F32)


def _split_bf16(x):
    hi = x.astype(BF16)
    lo = (x - hi.astype(F32)).astype(BF16)
    return hi, lo


def _params(n_axes, gathers=False):
    return pltpu.CompilerParams(dimension_semantics=("arbitrary",) * n_axes,
                                vmem_limit_bytes=VMEM_LIMIT_BYTES, disable_bounds_checks=gathers)


def _modulate(x, g, scale, shift):
    y = x * lax.rsqrt(jnp.mean(x * x, axis=-1, keepdims=True) + NORM_EPS) * g
    return y * (1.0 + scale) + shift


def _adaln_body(c_ref, w_ref, b_ref, o_ref):
    c = c_ref[...]
    a = (c * jax.nn.sigmoid(c)).astype(BF16)
    o_ref[...] = _dot(a, w_ref[...].astype(BF16)) + b_ref[...]


def _adaln(c_all, ada_w, ada_b):
    depth, d, n = ada_w.shape
    m = c_all.shape[0]
    tn = _pick(n, (1024, 512, 256, 128))
    return pl.pallas_call(
        _adaln_body,
        grid=(depth, n // tn),
        in_specs=[pl.BlockSpec((m, d), lambda l, j: (0, 0)),
                  pl.BlockSpec((None, d, tn), lambda l, j: (l, 0, j)),
                  pl.BlockSpec((None, 1, tn), lambda l, j: (l, 0, j))],
        out_specs=pl.BlockSpec((None, m, tn), lambda l, j: (l, 0, j)),
        out_shape=jax.ShapeDtypeStruct((depth, m, n), F32),
        compiler_params=_params(2),
        name="adaln",
    )(c_all, ada_w, ada_b.reshape(depth, 1, n))


class _Geom:
    def __init__(self, b, s, db, q, d):
        self.b, self.s, self.db, self.q, self.d = b, s, db, q, d
        self.tp, self.ts = b * s, db * q
        self.t = self.tp + self.ts
        rb = 256
        while s % rb or self.ts % rb or rb % q:
            rb //= 2
            if rb < 8:
                raise ValueError("token counts must be multiples of 8")
        self.rb = rb
        self.nbp, self.nbs = self.tp // rb, self.ts // rb
        self.nb = self.nbp + self.nbs

    def row(self, width, offset=0):
        return pl.BlockSpec((self.rb, width), lambda i: (i + offset, 0))

    def full(self, arr):
        nd = arr.ndim
        return pl.BlockSpec(arr.shape, lambda i: (0,) * nd)

    def mod(self, chunk, offset=0):
        nbp, spb, b = self.nbp, self.s // self.rb, self.b

        def index(i, *_):
            i = i + offset
            return (jnp.where(i < nbp, i // spb, b + i - nbp), chunk)

        return pl.BlockSpec((self.rb, self.d), index)

    def expand_mod(self, mod):
        return jnp.concatenate([jnp.repeat(mod[:self.b], self.rb, axis=0),
                                jnp.repeat(mod[self.b:self.b + self.db], self.q, axis=0)], axis=0)


def _swa_proj_body(x_ref, g_ref, sc_ref, sh_ref, w_ref, b_ref, gq_ref, gk_ref, e_ref,
                   q_ref, k_ref, v_ref, vt_ref, *, nq, nk, hd):
    h = _modulate(x_ref[...], g_ref[...], sc_ref[...], sh_ref[...]).astype(BF16)
    qkv = _dot(h, w_ref[...]) + b_ref[...]
    e = e_ref[...]
    inv_hd = 1.0 / hd
    for c in range(nq // nk):
        qc = qkv[:, c * nk:(c + 1) * nk]
        ss = _dot((qc * qc).astype(BF16), e)
        q_ref[:, c * nk:(c + 1) * nk] = (qc * lax.rsqrt(ss * inv_hd + NORM_EPS) * gq_ref[...]).astype(BF16)
    kc = qkv[:, nq:nq + nk]
    ss = _dot((kc * kc).astype(BF16), e)
    k_ref[...] = kc * lax.rsqrt(ss * inv_hd + NORM_EPS) * gk_ref[...]
    v = qkv[:, nq + nk:]
    v_ref[...] = v
    vt_ref[...] = v.T


def _swa_prompt_body(sink_ref, q_ref, kc_ref, kp_ref, vtc_ref, vtp_ref, bias_ref, o_ref, ot_ref, *, kv, grp, hd):
    i = pl.program_id(0)
    w = q_ref.shape[0]
    lanes = 2 * hd
    key_row = lax.broadcasted_iota(jnp.int32, (2 * w, 1), 0)
    no_prev = jnp.where(jnp.logical_and(i == 0, key_row < w), MASKED, 0.0)
    lane = lax.broadcasted_iota(jnp.int32, (1, lanes), 1)
    for g in range(kv):
        tile = slice((g // 2) * lanes, (g // 2 + 1) * lanes)
        kcat = jnp.concatenate([kp_ref[:, tile], kc_ref[:, tile]], axis=0)
        own = (lane >= hd) if g % 2 else (lane < hd)
        k_here = jnp.where(own, kcat, 0.0)
        k_at = [None, None]
        k_at[g % 2] = k_here.astype(BF16)
        k_at[1 - g % 2] = pltpu.roll(k_here, hd, axis=1).astype(BF16)
        rows = slice(g * hd, (g + 1) * hd)
        vt = jnp.concatenate([vtp_ref[rows, :], vtc_ref[rows, :]], axis=1).astype(BF16)
        for j in range(grp):
            h = g * grp + j
            q_pair = q_ref[:, (h // 2) * lanes:(h // 2 + 1) * lanes]
            s = _dot_nt(k_at[h % 2], q_pair) + bias_ref[h] + no_prev
            sink = sink_ref[h]
            m = jnp.maximum(jnp.max(s, axis=0, keepdims=True), sink)
            p = jnp.exp(s - m)
            den = jnp.sum(p, axis=0, keepdims=True) + jnp.exp(sink - m)
            ot_ref[h * hd:(h + 1) * hd, :] = _dot(vt, (p * (1.0 / den)).astype(BF16))
    o_ref[...] = ot_ref[...].T.astype(BF16)


def _swa_sample_body(sink_ref, q_ref, kn_ref, vn_ref, ks_ref, vs_ref, bias_old_ref, bias_new_ref,
                     o_ref, ko_ref, vo_ref, *, kv, grp, hd, nq_tok):
    nb, nbuf, _ = ks_ref.shape
    qt = nq_tok
    qf = q_ref[...].astype(F32)
    pad = jnp.zeros((nbuf - qt, hd), F32)
    for b in range(nb):
        rows = slice(b * qt, (b + 1) * qt)
        knew, vnew = kn_ref[rows, :], vn_ref[rows, :]
        ko_ref[b, :nbuf - qt, :] = ks_ref[b, qt:, :]
        ko_ref[b, nbuf - qt:, :] = knew
        vo_ref[b, :nbuf - qt, :] = vs_ref[b, qt:, :]
        vo_ref[b, nbuf - qt:, :] = vnew
        for g in range(kv):
            cs = slice(g * hd, (g + 1) * hd)
            k_old = ks_ref[b, :, cs].astype(BF16)
            v_old = vs_ref[b, :, cs].astype(BF16)
            k_new = jnp.concatenate([knew[:, cs], pad], axis=0).astype(BF16)
            v_new = jnp.concatenate([vnew[:, cs], pad], axis=0).astype(BF16)
            qg = jnp.concatenate([qf[rows, (g * grp + j) * hd:(g * grp + j + 1) * hd] for j in range(grp)],
                                 axis=0).astype(BF16)
            s_old = _dot_nt(qg, k_old) + bias_old_ref[g]
            s_new = _dot_nt(qg, k_new) + bias_new_ref[g]
            sink = jnp.concatenate([jnp.full((qt, 1), sink_ref[g * grp + j], F32) for j in range(grp)], axis=0)
            m = jnp.maximum(jnp.maximum(jnp.max(s_old, axis=-1, keepdims=True),
                                        jnp.max(s_new, axis=-1, keepdims=True)), sink)
            p_old = jnp.exp(s_old - m)
            p_new = jnp.exp(s_new - m)
            den = (jnp.sum(p_old, axis=-1, keepdims=True) + jnp.sum(p_new, axis=-1, keepdims=True)
                   + jnp.exp(sink - m))
            inv = 1.0 / den
            og = _dot((p_old * inv).astype(BF16), v_old) + _dot((p_new * inv).astype(BF16), v_new)
            for j in range(grp):
                o_ref[rows, (g * grp + j) * hd:(g * grp + j + 1) * hd] = og[j * qt:(j + 1) * qt, :]


def _alibi_bias(dist, valid, heads, kv):
    slopes = jnp.exp2(-ALIBI_MAX_BIAS * jnp.arange(1, heads + 1, dtype=F32) / heads)
    bias = jnp.where(valid[None], -slopes[:, None, None] * dist[None].astype(F32), MASKED)
    return bias.reshape(kv, (heads // kv) * dist.shape[0], dist.shape[1])


def _swa_layer(geom, x, modx, norm_g, w_qkv, b_qkv, g_q, g_k, sinks, w_o, b_o, state_k, state_v, past_len):
    d, rb = geom.d, geom.rb
    heads = sinks.shape[0]
    hd = g_q.shape[0]
    kv = state_k.shape[-2]
    grp = heads // kv
    nq, nk = heads * hd, kv * hd
    nbuf = state_k.shape[1]
    qt = geom.q
    if WINDOW != nbuf or geom.s % WINDOW:
        raise ValueError("window buffer must hold exactly one window and tile the prompt")
    if 2 * hd != 128 or kv % 2:
        raise ValueError("the prompt kernel pairs two heads per 128-lane tile")

    head_of = jnp.arange(nk) // hd
    e = (head_of[:, None] == head_of[None, :]).astype(BF16)
    gq_t = (jnp.tile(g_q, kv) * hd ** -0.5).reshape(1, nk)
    gk_t = jnp.tile(g_k, kv).reshape(1, nk)

    q, k, v, v_t = pl.pallas_call(
        functools.partial(_swa_proj_body, nq=nq, nk=nk, hd=hd),
        grid=(geom.nb,),
        in_specs=[geom.row(d), geom.full(norm_g), geom.mod(1), geom.mod(0),
                  pl.BlockSpec(w_qkv.shape, lambda i: (0, 0)), pl.BlockSpec((1, nq + 2 * nk), lambda i: (0, 0)),
                  geom.full(gq_t), geom.full(gk_t), geom.full(e)],
        out_specs=[geom.row(nq), geom.row(nk), geom.row(nk), pl.BlockSpec((nk, rb), lambda i: (0, i))],
        out_shape=[jax.ShapeDtypeStruct((geom.t, nq), BF16), jax.ShapeDtypeStruct((geom.t, nk), F32),
                   jax.ShapeDtypeStruct((geom.t, nk), F32), jax.ShapeDtypeStruct((nk, geom.t), F32)],
        compiler_params=_params(1),
        name="swa_proj",
    )(x, norm_g, modx, modx, w_qkv.astype(BF16), b_qkv.reshape(1, -1), gq_t, gk_t, e)

    w = WINDOW
    t_idx = jnp.arange(w)[None, :]
    s_idx = jnp.arange(2 * w)[:, None]
    dist = t_idx + w - s_idx
    slopes = jnp.exp2(-ALIBI_MAX_BIAS * jnp.arange(1, heads + 1, dtype=F32) / heads)
    bias_p = jnp.where(((dist >= 0) & (dist < WINDOW))[None], -slopes[:, None, None] * dist[None].astype(F32), MASKED)
    nblk = geom.tp // w
    bpb = geom.s // w

    def prev_rows(i):
        return (jnp.where(i % bpb == 0, i, i - 1), 0)

    def prev_cols(i):
        return (0, jnp.where(i % bpb == 0, i, i - 1))

    o_p = pl.pallas_call(
        functools.partial(_swa_prompt_body, kv=kv, grp=grp, hd=hd),
        grid=(nblk,),
        in_specs=[pl.BlockSpec(memory_space=pltpu.SMEM),
                  pl.BlockSpec((w, nq), lambda i: (i, 0)),
                  pl.BlockSpec((w, nk), lambda i: (i, 0)), pl.BlockSpec((w, nk), prev_rows),
                  pl.BlockSpec((nk, w), lambda i: (0, i)), pl.BlockSpec((nk, w), prev_cols),
                  pl.BlockSpec(bias_p.shape, lambda i: (0, 0, 0))],
        out_specs=pl.BlockSpec((w, nq), lambda i: (i, 0)),
        out_shape=jax.ShapeDtypeStruct((geom.tp, nq), BF16),
        scratch_shapes=[pltpu.VMEM((nq, w), F32)],
        compiler_params=_params(1),
        name="swa_prompt_attn",
    )(sinks, q, k, k, v_t, v_t, bias_p)

    qi = jnp.arange(qt)[:, None]
    so = jnp.arange(nbuf)[None, :]
    dist_old = qi + nbuf - so
    bias_old = _alibi_bias(dist_old, (dist_old < WINDOW) & (past_len - nbuf + so >= 0), heads, kv)
    dist_new = qi - so
    bias_new = _alibi_bias(dist_new, (dist_new >= 0) & (so < qt), heads, kv)
    nbt = _pick(geom.db, (8, 4, 2, 1))
    rows = nbt * qt
    off = geom.tp // rows
    o_s, k_s, v_s = pl.pallas_call(
        functools.partial(_swa_sample_body, kv=kv, grp=grp, hd=hd, nq_tok=qt),
        grid=(geom.db // nbt,),
        in_specs=[pl.BlockSpec(memory_space=pltpu.SMEM),
                  pl.BlockSpec((rows, nq), lambda i: (i + off, 0)),
                  pl.BlockSpec((rows, nk), lambda i: (i + off, 0)), pl.BlockSpec((rows, nk), lambda i: (i + off, 0)),
                  pl.BlockSpec((nbt, nbuf, nk), lambda i: (i, 0, 0)), pl.BlockSpec((nbt, nbuf, nk), lambda i: (i, 0, 0)),
                  pl.BlockSpec(bias_old.shape, lambda i: (0, 0, 0)), pl.BlockSpec(bias_new.shape, lambda i: (0, 0, 0))],
        out_specs=[pl.BlockSpec((rows, nq), lambda i: (i, 0)),
                   pl.BlockSpec((nbt, nbuf, nk), lambda i: (i, 0, 0)), pl.BlockSpec((nbt, nbuf, nk), lambda i: (i, 0, 0))],
        out_shape=[jax.ShapeDtypeStruct((geom.ts, nq), F32),
                   jax.ShapeDtypeStruct((geom.db, nbuf, nk), F32), jax.ShapeDtypeStruct((geom.db, nbuf, nk), F32)],
        compiler_params=_params(1),
        name="swa_sample_attn",
    )(sinks, q, k, v, state_k.reshape(geom.db, nbuf, nk), state_v.reshape(geom.db, nbuf, nk), bias_old, bias_new)

    o_all = jnp.concatenate([o_p, o_s.astype(BF16)], axis=0)
    x = _out_proj(geom, o_all, w_o, b_o, x, modx, 2)

    wbuf = min(WINDOW, geom.s)
    kp = k[:geom.tp].reshape(geom.b, geom.s, kv, hd)[:, geom.s - wbuf:]
    vp = v[:geom.tp].reshape(geom.b, geom.s, kv, hd)[:, geom.s - wbuf:]
    return x, kp, vp, k_s.reshape(geom.db, nbuf, kv, hd), v_s.reshape(geom.db, nbuf, kv, hd)


def _out_proj_body(o_ref, w_ref, b_ref, x_ref, gate_ref, out_ref):
    y = _dot(o_ref[...], w_ref[...]) + b_ref[...]
    out_ref[...] = x_ref[...] + gate_ref[...] * y


def _out_proj_nobias_body(o_ref, w_ref, x_ref, gate_ref, out_ref):
    out_ref[...] = x_ref[...] + gate_ref[...] * _dot(o_ref[...], w_ref[...])


def _out_proj(geom, o_all, w_o, b_o, x, modx, gate_chunk):
    d = geom.d
    w_bf = w_o.astype(BF16)
    if b_o is None:
        body, extra, extra_specs = _out_proj_nobias_body, (), []
    else:
        body, extra, extra_specs = _out_proj_body, (b_o.reshape(1, d),), [pl.BlockSpec((1, d), lambda i: (0, 0))]
    return pl.pallas_call(
        body,
        grid=(geom.nb,),
        in_specs=[geom.row(o_all.shape[1]), geom.full(w_bf)] + extra_specs + [geom.row(d), geom.mod(gate_chunk)],
        out_specs=geom.row(d),
        out_shape=jax.ShapeDtypeStruct((geom.t, d), F32),
        compiler_params=_params(1),
        name="out_proj",
    )(o_all, w_bf, *extra, x, modx)


def _moe_prep_body(x_ref, g_ref, sc_ref, sh_ref, wr_hi_ref, wr_lo_ref, br_ref, h_ref, lg_ref):
    h = _modulate(x_ref[...], g_ref[...], sc_ref[...], sh_ref[...])
    h_hi, h_lo = _split_bf16(h)
    wr_hi = wr_hi_ref[...]
    lg_ref[...] = _dot(h_hi, wr_hi) + _dot(h_hi, wr_lo_ref[...]) + _dot(h_lo, wr_hi) + br_ref[...]
    h_ref[...] = h


def _row_copy(src_hbm, src_row, dst_vmem, dst_row, sem):
    return pltpu.make_async_copy(src_hbm.at[pl.ds(src_row, 1), :], dst_vmem.at[pl.ds(dst_row, 1), :], sem)


def _gather_rows_start(idx_ref, base, n_rows, src_hbm, dst_vmem, sem):
    def body(g, carry):
        for k in range(GATHER_UNROLL):
            r = g * GATHER_UNROLL + k
            _row_copy(src_hbm, idx_ref[base + r], dst_vmem, r, sem).start()
        return carry

    lax.fori_loop(0, n_rows // GATHER_UNROLL, body, 0)


def _gather_rows_wait(n_rows, src_hbm, dst_vmem, sem):
    def body(g, carry):
        for k in range(GATHER_UNROLL):
            _row_copy(src_hbm, 0, dst_vmem, 0, sem).wait()
        return carry

    lax.fori_loop(0, n_rows // GATHER_UNROLL, body, 0)


def _moe_expert_body(be_ref, nvalid_ref, tok_ref, h_hbm, wg_ref, wu_ref, bg_ref, bu_ref, wd_ref, bd_ref, gate_ref,
                     out_ref, xbuf, xbf, sem):
    blk, n = pl.program_id(0), pl.program_id(1)
    last = pl.num_programs(1) - 1
    n_valid = nvalid_ref[0]
    valid = blk < n_valid
    tm = xbf.shape[0]

    @pl.when(n == 0)
    def _():
        @pl.when(blk == 0)
        def _():
            _gather_rows_start(tok_ref, 0, tm, h_hbm, xbuf, sem.at[0])

        @pl.when(valid)
        def _():
            _gather_rows_wait(tm, h_hbm, xbuf, sem.at[0])
            xbf[...] = xbuf[...].astype(BF16)
            out_ref[...] = jnp.broadcast_to(bd_ref[...], out_ref.shape)

        @pl.when(blk + 1 < n_valid)
        def _():
            _gather_rows_start(tok_ref, (blk + 1) * tm, tm, h_hbm, xbuf, sem.at[0])

    @pl.when(valid)
    def _():
        x = xbf[...]
        gl = jnp.minimum(_dot(x, wg_ref[...].astype(BF16)) + bg_ref[...], SWIGLU_LIMIT)
        up = jnp.clip(_dot(x, wu_ref[...].astype(BF16)) + bu_ref[...], -SWIGLU_LIMIT, SWIGLU_LIMIT)
        act = (up + 1.0) * gl * jax.nn.sigmoid(SWIGLU_ALPHA * gl)
        out_ref[...] += _dot(act.astype(BF16), wd_ref[...].astype(BF16))

        @pl.when(n == last)
        def _():
            out_ref[...] *= gate_ref[...]

    @pl.when(jnp.logical_and(jnp.logical_not(valid), n == 0))
    def _():
        out_ref[...] = jnp.zeros_like(out_ref)


def _moe_layer(geom, layer, x, modx, norm_g, w_r, b_r, w_gu, b_gu, w_d, b_d):
    d, t = geom.d, geom.t
    n_exp = w_r.shape[-1]
    de = w_d.shape[2]
    lanes = 128
    e_pad = -(-n_exp // lanes) * lanes
    wr_hi, wr_lo = _split_bf16(jnp.pad(w_r, ((0, 0), (0, e_pad - n_exp))))
    br = jnp.pad(b_r, (0, e_pad - n_exp)).reshape(1, e_pad)

    h, logits = pl.pallas_call(
        _moe_prep_body,
        grid=(geom.nb,),
        in_specs=[geom.row(d), geom.full(norm_g), geom.mod(4), geom.mod(3),
                  geom.full(wr_hi), geom.full(wr_lo), geom.full(br)],
        out_specs=[geom.row(d), geom.row(e_pad)],
        out_shape=[jax.ShapeDtypeStruct((t, d), F32), jax.ShapeDtypeStruct((t, e_pad), F32)],
        compiler_params=_params(1),
        name="moe_prep",
    )(x, norm_g, modx, modx, wr_hi, wr_lo, br)

    tm = EXPERT_ROWS
    top_v, top_e = lax.top_k(logits[:, :n_exp], TOP_K)
    gates = jax.nn.softmax(top_v, axis=-1).reshape(-1)
    e_flat = top_e.reshape(-1).astype(jnp.int32)
    n_assign = t * TOP_K
    order = jnp.argsort(e_flat).astype(jnp.int32)
    rank = jnp.argsort(order).astype(jnp.int32)
    bounds = jnp.searchsorted(e_flat[order], jnp.arange(n_exp + 1, dtype=jnp.int32), side='left').astype(jnp.int32)
    cum, counts = bounds[:-1], bounds[1:] - bounds[:-1]
    padded = (counts + tm - 1) // tm * tm
    pad_end = jnp.cumsum(padded)
    start = pad_end - padded
    n_blocks = -(-(n_assign + n_exp * (tm - 1)) // tm)
    n_slots = n_blocks * tm
    blk_e = jnp.minimum(jnp.searchsorted(pad_end, jnp.arange(n_blocks, dtype=jnp.int32) * tm, side='right'),
                        n_exp - 1).astype(jnp.int32)
    n_valid = (pad_end[-1] // tm).astype(jnp.int32).reshape(1)
    blk_i = jnp.arange(n_blocks, dtype=jnp.int32)
    r_s = (blk_i * tm - start[blk_e])[:, None] + jnp.arange(tm, dtype=jnp.int32)[None, :]
    live = (r_s < counts[blk_e][:, None]) & (blk_i < n_valid[0])[:, None]
    a_s = order[jnp.clip(cum[blk_e][:, None] + r_s, 0, n_assign - 1)]
    tok = jnp.where(live, a_s // TOP_K, 0).astype(jnp.int32).reshape(-1)
    gate = jnp.where(live, gates[a_s], 0.0).reshape(-1)
    slot_of = (start[e_flat] + rank - cum[e_flat]).astype(jnp.int32)

    tn = _pick(de, (EXPERT_COLS, 256, 128))
    nt = de // tn
    out = pl.pallas_call(
        _moe_expert_body,
        grid_spec=pltpu.PrefetchScalarGridSpec(
            num_scalar_prefetch=3,
            grid=(n_blocks, nt),
            in_specs=[pl.BlockSpec(memory_space=pl.ANY),
                      pl.BlockSpec((None, None, d, tn), lambda b, n, be, nv, tk: (layer, be[b], 0, n)),
                      pl.BlockSpec((None, None, d, tn), lambda b, n, be, nv, tk: (layer, be[b], 0, nt + n)),
                      pl.BlockSpec((None, None, 1, tn), lambda b, n, be, nv, tk: (layer, be[b], 0, n)),
                      pl.BlockSpec((None, None, 1, tn), lambda b, n, be, nv, tk: (layer, be[b], 0, nt + n)),
                      pl.BlockSpec((None, None, tn, d), lambda b, n, be, nv, tk: (layer, be[b], n, 0)),
                      pl.BlockSpec((None, None, 1, d), lambda b, n, be, nv, tk: (layer, be[b], 0, 0)),
                      pl.BlockSpec((tm, 1), lambda b, n, be, nv, tk: (b, 0))],
            out_specs=pl.BlockSpec((tm, d), lambda b, n, be, nv, tk: (b, 0)),
            scratch_shapes=[pltpu.VMEM((tm, d), F32), pltpu.VMEM((tm, d), BF16), pltpu.SemaphoreType.DMA((1,))]),
        out_shape=jax.ShapeDtypeStruct((n_slots, d), F32),
        compiler_params=_params(2, gathers=True),
        name="moe_experts",
    )(blk_e, n_valid, tok, h, w_gu, w_gu, b_gu.reshape(*b_gu.shape[:2], 1, -1), b_gu.reshape(*b_gu.shape[:2], 1, -1),
      w_d, b_d.reshape(*b_d.shape[:2], 1, -1), gate.reshape(n_slots, 1))

    return _moe_combine(geom, x, out, slot_of, modx, 5)


def _moe_combine_body(slot_ref, out_hbm, x_ref, gate_ref, o_ref, ybuf, sem, *, top_k):
    i = pl.program_id(0)
    rb = x_ref.shape[0]
    n_rows = top_k * rb
    slot = i % 2

    @pl.when(i == 0)
    def _():
        _gather_rows_start(slot_ref, 0, n_rows, out_hbm, ybuf.at[0], sem.at[0])

    @pl.when(i + 1 < pl.num_programs(0))
    def _():
        _gather_rows_start(slot_ref, (i + 1) * n_rows, n_rows, out_hbm, ybuf.at[1 - slot], sem.at[1 - slot])

    _gather_rows_wait(n_rows, out_hbm, ybuf.at[slot], sem.at[slot])
    y = ybuf[slot, 0:rb, :]
    for k in range(1, top_k):
        y = y + ybuf[slot, k * rb:(k + 1) * rb, :]
    o_ref[...] = x_ref[...] + gate_ref[...] * y


def _moe_combine(geom, x, out, slot_of, modx, gate_chunk):
    d = geom.d
    return pl.pallas_call(
        functools.partial(_moe_combine_body, top_k=TOP_K),
        grid_spec=pltpu.PrefetchScalarGridSpec(
            num_scalar_prefetch=1,
            grid=(geom.nb,),
            in_specs=[pl.BlockSpec(memory_space=pl.ANY),
                      pl.BlockSpec((geom.rb, d), lambda i, sl: (i, 0)),
                      geom.mod(gate_chunk)],
            out_specs=pl.BlockSpec((geom.rb, d), lambda i, sl: (i, 0)),
            scratch_shapes=[pltpu.VMEM((2, TOP_K * geom.rb, d), F32), pltpu.SemaphoreType.DMA((2,))]),
        out_shape=jax.ShapeDtypeStruct((geom.t, d), F32),
        compiler_params=_params(1, gathers=True),
        name="moe_combine",
    )(slot_of.reshape(geom.nb, geom.rb, TOP_K).transpose(0, 2, 1).reshape(-1), out, x, modx)


def _rope_tables(pos, half):
    inv = ROPE_THETA ** (-jnp.arange(half, dtype=F32) / half)
    ang = pos.astype(F32)[:, None] * inv
    return jnp.cos(ang), jnp.sin(ang)


def _mla_down_body(x_ref, g_ref, sc_ref, sh_ref, w_ref, gqa_ref, gkva_ref,
                   cq_ref, ckv_ref, kr_ref, krs_ref, *, ql, kl, rope):
    h = _modulate(x_ref[...], g_ref[...], sc_ref[...], sh_ref[...]).astype(BF16)
    dd = _dot(h, w_ref[...])
    dq = dd[:, :ql]
    cq_ref[...] = (dq * lax.rsqrt(jnp.mean(dq * dq, axis=-1, keepdims=True) + NORM_EPS) * gqa_ref[...]).astype(BF16)
    dk = dd[:, ql:ql + kl]
    ckv_ref[...] = dk * lax.rsqrt(jnp.mean(dk * dk, axis=-1, keepdims=True) + NORM_EPS) * gkva_ref[...]
    kr_ref[...] = dd[:, ql + kl:ql + kl + rope]
    krs_ref[...] = dd[:, ql + kl + rope:]


def _mla_q_prompt_body(cq_ref, w_ref, a_ref, b_ref, q_ref, *, heads, hw, qk):
    cq = cq_ref[...]
    a, bt = a_ref[...], b_ref[...]
    for h in range(heads):
        x = _dot(cq, w_ref[:, h * hw:(h + 1) * hw])
        xs = _dot(cq, w_ref[:, (heads + h) * hw:(heads + h + 1) * hw])
        r = lax.rsqrt(jnp.sum(x * x, axis=-1, keepdims=True) * (1.0 / qk) + NORM_EPS)
        q_ref[:, h * hw:(h + 1) * hw] = ((x * a + xs * bt) * r).astype(BF16)


def _mla_q_sample_body(cq_ref, w_ref, a_ref, b_ref, a2_ref, b2_ref, wuk_ref, qabs_ref, qr_ref,
                       *, heads, hw, qk, nope, rope, nq_tok):
    cq = cq_ref[...]
    rb = cq.shape[0]
    nb = rb // nq_tok
    a, bt = a_ref[...], b_ref[...]
    a2, b2 = a2_ref[...], b2_ref[...]
    for h in range(heads):
        x = _dot(cq, w_ref[:, h * hw:(h + 1) * hw])
        xs = _dot(cq, w_ref[:, (heads + h) * hw:(heads + h + 1) * hw])
        r = lax.rsqrt(jnp.sum(x * x, axis=-1, keepdims=True) * (1.0 / qk) + NORM_EPS)
        qn = (x[:, :nope] * a[:, :nope] * r).astype(BF16)
        qabs = _dot_nt(qn, wuk_ref[:, h * nope:(h + 1) * nope])
        xr, xsr = x[:, nope:nope + rope], xs[:, nope:nope + rope]
        q1 = (xr * a[:, nope:nope + rope] + xsr * bt[:, nope:nope + rope]) * r
        q2 = (xsr * a2 + xr * b2) * r
        rows = slice(h * nq_tok, (h + 1) * nq_tok)
        qabs_ref[:, rows, :] = qabs.reshape(nb, nq_tok, qabs.shape[1])
        qr_ref[:, rows, :rope] = q1.reshape(nb, nq_tok, rope)
        qr_ref[:, rows, rope:] = q2.reshape(nb, nq_tok, rope)


def _mla_kv_prompt_body(ckv_ref, kr_ref, krs_ref, wuk_ref, wuv_ref, gkn_ref, ak_ref, bk_ref, k_ref, v_ref,
                        *, heads, hw, qk, nope, rope, vd):
    c = ckv_ref[...].astype(BF16)
    kr = kr_ref[...]
    ssr = jnp.sum(kr * kr, axis=-1, keepdims=True)
    krot = kr * ak_ref[...] + krs_ref[...] * bk_ref[...]
    gkn = gkn_ref[...]
    zeros = jnp.zeros((c.shape[0], hw - nope - rope), BF16)
    for h in range(heads):
        kn = _dot(c, wuk_ref[:, h * nope:(h + 1) * nope])
        r = lax.rsqrt((jnp.sum(kn * kn, axis=-1, keepdims=True) + ssr) * (1.0 / qk) + NORM_EPS)
        k_ref[:, h * hw:h * hw + nope] = (kn * gkn * r).astype(BF16)
        k_ref[:, h * hw + nope:h * hw + nope + rope] = (krot * r).astype(BF16)
        k_ref[:, h * hw + nope + rope:(h + 1) * hw] = zeros
        v_ref[h * vd:(h + 1) * vd, :] = _dot_nt(wuv_ref[h * vd:(h + 1) * vd, :], c).astype(BF16)


def _flash_body(qi_ref, kj_ref, q_ref, k_ref, v_ref, o_ref, m_ref, l_ref, acc_ref):
    n = pl.program_id(1)
    qi, kj = qi_ref[n], kj_ref[n]
    tq, tk = q_ref.shape[0], k_ref.shape[0]

    @pl.when(kj == 0)
    def _():
        m_ref[...] = jnp.full_like(m_ref, MASKED)
        l_ref[...] = jnp.zeros_like(l_ref)
        acc_ref[...] = jnp.zeros_like(acc_ref)

    s = _dot_nt(k_ref[...], q_ref[...])
    keys = kj * tk + lax.broadcasted_iota(jnp.int32, (tk, 1), 0)
    queries = qi * tq + lax.broadcasted_iota(jnp.int32, (1, tq), 1)
    s = jnp.where(keys <= queries, s, MASKED)
    m_old = m_ref[...]
    m_new = jnp.maximum(m_old, jnp.max(s, axis=0, keepdims=True))
    p = jnp.exp(s - m_new)
    alpha = jnp.exp(m_old - m_new)
    l_ref[...] = l_ref[...] * alpha + jnp.sum(p, axis=0, keepdims=True)
    acc_ref[...] = acc_ref[...] * alpha + _dot(v_ref[...], p.astype(BF16))
    m_ref[...] = m_new

    @pl.when(kj == qi)
    def _():
        o_ref[...] = (acc_ref[...] * (1.0 / l_ref[...])).T.astype(BF16)


def _mla_prompt_attn(q_p, k_p, v_p, seq_index, s, heads, hw, vd):
    tq = _pick(s, (FLASH_ROWS, 512, 256, 128, 64))
    nqb = s // tq
    pairs = [(i, j) for i in range(nqb) for j in range(i + 1)]
    qi_tab = jnp.asarray([p[0] for p in pairs], jnp.int32)
    kj_tab = jnp.asarray([p[1] for p in pairs], jnp.int32)
    off = seq_index * nqb
    return pl.pallas_call(
        _flash_body,
        grid_spec=pltpu.PrefetchScalarGridSpec(
            num_scalar_prefetch=2,
            grid=(heads, len(pairs)),
            in_specs=[pl.BlockSpec((tq, hw), lambda h, n, qi, kj: (off + qi[n], h)),
                      pl.BlockSpec((tq, hw), lambda h, n, qi, kj: (off + kj[n], h)),
                      pl.BlockSpec((vd, tq), lambda h, n, qi, kj: (h, off + kj[n]))],
            out_specs=pl.BlockSpec((tq, vd), lambda h, n, qi, kj: (qi[n], h)),
            scratch_shapes=[pltpu.VMEM((1, tq), F32), pltpu.VMEM((1, tq), F32), pltpu.VMEM((vd, tq), F32)]),
        out_shape=jax.ShapeDtypeStruct((s, heads * vd), BF16),
        compiler_params=_params(2),
        name="mla_prompt_attn",
    )(qi_tab, kj_tab, q_p, k_p, v_p)


def _mla_sample_body(pt_ref, qabs_ref, qr_ref, cnew_ref, krnew_ref, cs_ref, csn_ref, wukt_ref, ckv_hbm, krt_hbm,
                     out_ref, cbuf, krbuf, sem_c, sem_k, qabs_bf, qr_bf, m_ref, l_ref, acc_ref,
                     *, layer, heads, nq_tok, nope, qk, pages, chunk_pages):
    b, j = pl.program_id(0), pl.program_id(1)
    nj = pl.num_programs(1)
    step = b * nj + j
    last_step = pl.num_programs(0) * nj - 1
    slot = step % 2
    hq = heads * nq_tok
    page = cbuf.shape[2]

    def page_copies(step_, slot_, u):
        phys = pt_ref[step_ * pages + u]
        return (pltpu.make_async_copy(ckv_hbm.at[layer, phys], cbuf.at[slot_, u], sem_c.at[slot_]),
                pltpu.make_async_copy(krt_hbm.at[layer, phys], krbuf.at[slot_, u], sem_k.at[slot_]))

    def start_pages(step_, slot_, us):
        for u in us:
            for cp in page_copies(step_, slot_, u):
                cp.start()

    def wait_pages(step_, slot_):
        for u in range(pages):
            for cp in page_copies(step_, slot_, u):
                cp.wait()

    @pl.when(step == 0)
    def _():
        start_pages(0, 0, range(pages))

    wait_pages(step, slot)
    next_step = jnp.minimum(step + 1, last_step)

    @pl.when(j == 0)
    def _():
        m_ref[...] = jnp.full_like(m_ref, MASKED)
        l_ref[...] = jnp.zeros_like(l_ref)
        acc_ref[...] = jnp.zeros_like(acc_ref)
        qabs_bf[...] = qabs_ref[...].astype(BF16)
        qr_bf[...] = qr_ref[...].astype(BF16)

    def scores(c_bf, kr_t, cs):
        n = c_bf.shape[0]
        ssn_parts = []
        for h0 in range(0, heads, SAMPLE_HEAD_GROUP):
            kt = _dot_nt(wukt_ref[h0 * nope:(h0 + SAMPLE_HEAD_GROUP) * nope, :], c_bf)
            ssn_parts.append(jnp.sum((kt * kt).reshape(SAMPLE_HEAD_GROUP, nope, n), axis=1))
        ssn = jnp.concatenate(ssn_parts, axis=0)
        ssr = jnp.sum(kr_t * kr_t, axis=0, keepdims=True)
        rk = lax.rsqrt((ssn + ssr) * (1.0 / qk) + NORM_EPS)
        k_rot = (jnp.concatenate([kr_t, kr_t], axis=0) * cs).astype(BF16)
        s = _dot_nt(qabs_bf[...], c_bf) + _dot(qr_bf[...], k_rot)
        return s * jnp.broadcast_to(rk[:, None, :], (heads, nq_tok, n)).reshape(hq, n)

    def update(s, c_bf):
        m_old = m_ref[...]
        m_new = jnp.maximum(m_old, jnp.max(s, axis=-1, keepdims=True))
        p = jnp.exp(s - m_new)
        alpha = jnp.exp(m_old - m_new)
        l_ref[...] = l_ref[...] * alpha + jnp.sum(p, axis=-1, keepdims=True)
        acc_ref[...] = acc_ref[...] * alpha + _dot(p.astype(BF16), c_bf)
        m_ref[...] = m_new

    s_parts, c_parts = [], []
    for u in range(0, pages, chunk_pages):
        us = range(u, u + chunk_pages)
        c_bf = jnp.concatenate([cbuf[slot, v].astype(BF16) for v in us], axis=0)
        kr_t = jnp.concatenate([krbuf[slot, v] for v in us], axis=1)
        s_parts.append(scores(c_bf, kr_t, cs_ref[:, u * page:(u + chunk_pages) * page]))
        c_parts.append(c_bf)
        start_pages(next_step, 1 - slot, us)
    update(jnp.concatenate(s_parts, axis=1), jnp.concatenate(c_parts, axis=0))

    @pl.when(j == nj - 1)
    def _():
        n_new = cnew_ref.shape[0]
        row_q = lax.broadcasted_iota(jnp.int32, (hq, n_new), 0) % nq_tok
        col = lax.broadcasted_iota(jnp.int32, (hq, n_new), 1)
        c_bf = cnew_ref[...].astype(BF16)
        s = scores(c_bf, krnew_ref[...], csn_ref[...])
        update(jnp.where(col <= row_q, s, MASKED), c_bf)
        out_ref[...] = acc_ref[...] * (1.0 / l_ref[...])

    @pl.when(step == last_step)
    def _():
        wait_pages(last_step, 1 - slot)


def _mla_sample_attn(layer, page_table, qabs, qr, cnew, krnew_t, cache_ckv, cache_kr_t, cs_old, cs_new, w_uk_t,
                     *, heads, nq_tok, nope, qk):
    db, hq, kl = qabs.shape
    n_pages = page_table.shape[1]
    page = cache_ckv.shape[2]
    rope = cache_kr_t.shape[2]
    n_new = cnew.shape[1]
    pg = _pick(n_pages, (PAGES_PER_STEP, 8, 4, 2, 1))
    chunk_pages = _pick(pg, (SAMPLE_CHUNK_PAGES, 2, 1))
    return pl.pallas_call(
        functools.partial(_mla_sample_body, layer=layer, heads=heads, nq_tok=nq_tok, nope=nope, qk=qk, pages=pg,
                          chunk_pages=chunk_pages),
        grid_spec=pltpu.PrefetchScalarGridSpec(
            num_scalar_prefetch=1,
            grid=(db, n_pages // pg),
            in_specs=[pl.BlockSpec((None, hq, kl), lambda b, j, pt: (b, 0, 0)),
                      pl.BlockSpec((None, hq, 2 * rope), lambda b, j, pt: (b, 0, 0)),
                      pl.BlockSpec((None, n_new, kl), lambda b, j, pt: (b, 0, 0)),
                      pl.BlockSpec((None, rope, n_new), lambda b, j, pt: (b, 0, 0)),
                      pl.BlockSpec((2 * rope, pg * page), lambda b, j, pt: (0, j)),
                      pl.BlockSpec((2 * rope, n_new), lambda b, j, pt: (0, 0)),
                      pl.BlockSpec(w_uk_t.shape, lambda b, j, pt: (0, 0)),
                      pl.BlockSpec(memory_space=pl.ANY),
                      pl.BlockSpec(memory_space=pl.ANY)],
            out_specs=pl.BlockSpec((None, hq, kl), lambda b, j, pt: (b, 0, 0)),
            scratch_shapes=[pltpu.VMEM((2, pg, page, kl), F32), pltpu.VMEM((2, pg, rope, page), F32),
                            pltpu.SemaphoreType.DMA((2,)), pltpu.SemaphoreType.DMA((2,)),
                            pltpu.VMEM((hq, kl), BF16), pltpu.VMEM((hq, 2 * rope), BF16),
                            pltpu.VMEM((hq, 1), F32), pltpu.VMEM((hq, 1), F32), pltpu.VMEM((hq, kl), F32)]),
        out_shape=jax.ShapeDtypeStruct((db, hq, kl), F32),
        compiler_params=_params(2, gathers=True),
        name="mla_sample_attn",
    )(page_table.reshape(-1), qabs, qr, cnew, krnew_t, cs_old, cs_new, w_uk_t, cache_ckv, cache_kr_t)


def _mla_uv_body(lat_ref, wuv_ref, o_ref, *, heads, nq_tok, vd):
    nb, _, kl = lat_ref.shape
    for h in range(heads):
        lat = lat_ref[:, h * nq_tok:(h + 1) * nq_tok, :].reshape(nb * nq_tok, kl).astype(BF16)
        o_ref[:, h * vd:(h + 1) * vd] = _dot(lat, wuv_ref[:, h * vd:(h + 1) * vd])


def _mla_layer(geom, layer, x, modx, norm_g, w_dkv, g_qa, g_kva, w_uq, g_q, w_uk, g_k, w_uv, w_o,
               cache_ckv, cache_kr, page_table, past_len):
    d, rb, qt = geom.d, geom.rb, geom.q
    ql, kl = g_qa.shape[0], g_kva.shape[0]
    rope = cache_kr.shape[-1]
    half = rope // 2
    qk = g_q.shape[0]
    nope = qk - rope
    heads = w_uq.shape[1] // qk
    vd = w_uv.shape[1] // heads
    hw = MXU_WIDTH
    scale = qk ** -0.5
    page = cache_ckv.shape[2]
    n_pages = page_table.shape[1]
    tp, ts = geom.tp, geom.ts

    w_dn = jnp.concatenate([w_dkv, w_dkv[:, ql + kl + half:], w_dkv[:, ql + kl:ql + kl + half]], axis=1).astype(BF16)
    gqa, gkva = g_qa.reshape(1, ql), g_kva.reshape(1, kl)
    cq, ckv, kr, krs = pl.pallas_call(
        functools.partial(_mla_down_body, ql=ql, kl=kl, rope=rope),
        grid=(geom.nb,),
        in_specs=[geom.row(d), geom.full(norm_g), geom.mod(1), geom.mod(0), geom.full(w_dn),
                  geom.full(gqa), geom.full(gkva)],
        out_specs=[geom.row(ql), geom.row(kl), geom.row(rope), geom.row(rope)],
        out_shape=[jax.ShapeDtypeStruct((geom.t, ql), BF16), jax.ShapeDtypeStruct((geom.t, kl), F32),
                   jax.ShapeDtypeStruct((geom.t, rope), F32), jax.ShapeDtypeStruct((geom.t, rope), F32)],
        compiler_params=_params(1),
        name="mla_down",
    )(x, norm_g, modx, modx, w_dn, gqa, gkva)

    w3 = w_uq.reshape(ql, heads, qk)
    wn, w1, w2 = w3[:, :, :nope], w3[:, :, nope:nope + half], w3[:, :, nope + half:]
    zpad = jnp.zeros((ql, heads, hw - qk), F32)
    w_q = jnp.concatenate([jnp.concatenate([wn, w1, w2, zpad], axis=-1).reshape(ql, heads * hw),
                           jnp.concatenate([jnp.zeros_like(wn), w2, w1, zpad], axis=-1).reshape(ql, heads * hw)],
                          axis=1).astype(BF16)
    gqn, gq1, gq2 = g_q[:nope], g_q[nope:nope + half], g_q[nope + half:]
    gkn, gk1, gk2 = g_k[:nope], g_k[nope:nope + half], g_k[nope + half:]

    def q_tables(pos, nope_gain):
        cos, sin = _rope_tables(pos, half)
        n = pos.shape[0]
        z = jnp.zeros((n, hw - qk), F32)
        a = jnp.concatenate([jnp.broadcast_to(nope_gain, (n, nope)), gq1 * cos, gq2 * cos, z], axis=1) * scale
        b = jnp.concatenate([jnp.zeros((n, nope), F32), -gq2 * sin, gq1 * sin, z], axis=1) * scale
        return a, b, cos, sin

    pos_p = jnp.tile(jnp.arange(geom.s, dtype=jnp.int32), geom.b)
    pos_s = jnp.tile(past_len + jnp.arange(qt, dtype=jnp.int32), geom.db)
    a_p, b_p, cos_p, sin_p = q_tables(pos_p, gqn)
    a_s, b_s, cos_s, sin_s = q_tables(pos_s, gqn * gkn)
    a2_s = jnp.concatenate([gq2 * cos_s, -gq1 * cos_s], axis=1) * scale
    b2_s = jnp.concatenate([gq1 * sin_s, gq2 * sin_s], axis=1) * scale

    q_p = pl.pallas_call(
        functools.partial(_mla_q_prompt_body, heads=heads, hw=hw, qk=qk),
        grid=(geom.nbp,),
        in_specs=[geom.row(ql), geom.full(w_q), geom.row(hw), geom.row(hw)],
        out_specs=geom.row(heads * hw),
        out_shape=jax.ShapeDtypeStruct((tp, heads * hw), BF16),
        compiler_params=_params(1),
        name="mla_q_prompt",
    )(cq, w_q, a_p, b_p)

    w_uk_bf = w_uk.astype(BF16)
    nbq = rb // qt
    hq = heads * qt
    qabs, qr = pl.pallas_call(
        functools.partial(_mla_q_sample_body, heads=heads, hw=hw, qk=qk, nope=nope, rope=rope, nq_tok=qt),
        grid=(geom.nbs,),
        in_specs=[geom.row(ql, geom.nbp), geom.full(w_q), geom.row(hw), geom.row(hw), geom.row(rope), geom.row(rope),
                  geom.full(w_uk_bf)],
        out_specs=[pl.BlockSpec((nbq, hq, kl), lambda i: (i, 0, 0)),
                   pl.BlockSpec((nbq, hq, 2 * rope), lambda i: (i, 0, 0))],
        out_shape=[jax.ShapeDtypeStruct((geom.db, hq, kl), F32), jax.ShapeDtypeStruct((geom.db, hq, 2 * rope), F32)],
        compiler_params=_params(1),
        name="mla_q_sample",
    )(cq, w_q, a_s, b_s, a2_s, b2_s, w_uk_bf)

    w_uv_bf = w_uv.astype(BF16)
    a_k = jnp.concatenate([gk1 * cos_p, gk2 * cos_p], axis=1)
    b_k = jnp.concatenate([-gk2 * sin_p, gk1 * sin_p], axis=1)
    gkn_row = gkn.reshape(1, nope)
    w_uv_t = w_uv.T.astype(BF16)
    k_p, v_p = pl.pallas_call(
        functools.partial(_mla_kv_prompt_body, heads=heads, hw=hw, qk=qk, nope=nope, rope=rope, vd=vd),
        grid=(geom.nbp,),
        in_specs=[geom.row(kl), geom.row(rope), geom.row(rope), geom.full(w_uk_bf), geom.full(w_uv_t),
                  geom.full(gkn_row), geom.row(rope), geom.row(rope)],
        out_specs=[geom.row(heads * hw), pl.BlockSpec((heads * vd, rb), lambda i: (0, i))],
        out_shape=[jax.ShapeDtypeStruct((tp, heads * hw), BF16), jax.ShapeDtypeStruct((heads * vd, tp), BF16)],
        compiler_params=_params(1),
        name="mla_kv_prompt",
    )(ckv, kr, krs, w_uk_bf, w_uv_t, gkn_row, a_k, b_k)

    o_parts = [_mla_prompt_attn(q_p, k_p, v_p, bi, geom.s, heads, hw, vd) for bi in range(geom.b)]

    n_new = 128
    ckv_s = ckv[tp:].reshape(geom.db, qt, kl)
    kr_s = kr[tp:].reshape(geom.db, qt, rope)
    cnew = jnp.pad(ckv_s, ((0, 0), (0, n_new - qt), (0, 0)))
    krnew_t = jnp.pad(kr_s, ((0, 0), (0, n_new - qt), (0, 0))).transpose(0, 2, 1)
    cos_c, sin_c = _rope_tables(jnp.arange(past_len + n_new, dtype=jnp.int32), half)
    cs = jnp.concatenate([gk1 * cos_c, gk2 * cos_c, gk1 * sin_c, gk2 * sin_c], axis=1).T
    o_lat = _mla_sample_attn(layer, page_table, qabs, qr, cnew, krnew_t, cache_ckv, cache_kr.transpose(0, 1, 3, 2),
                             cs[:, :past_len], cs[:, past_len:], w_uk.T.astype(BF16),
                             heads=heads, nq_tok=qt, nope=nope, qk=qk)

    o_s = pl.pallas_call(
        functools.partial(_mla_uv_body, heads=heads, nq_tok=qt, vd=vd),
        grid=(geom.nbs,),
        in_specs=[pl.BlockSpec((nbq, hq, kl), lambda i: (i, 0, 0)), geom.full(w_uv_bf)],
        out_specs=geom.row(heads * vd),
        out_shape=jax.ShapeDtypeStruct((ts, heads * vd), F32),
        compiler_params=_params(1),
        name="mla_uv",
    )(o_lat, w_uv_bf)

    o_all = jnp.concatenate(o_parts + [o_s.astype(BF16)], axis=0)
    x = _out_proj(geom, o_all, w_o, None, x, modx, 2)
    return (x, ckv[:tp].reshape(geom.b, geom.s, kl), kr[:tp].reshape(geom.b, geom.s, rope), ckv_s, kr_s)


def kernel(x_prompt, x_sample, state_swa_k, state_swa_v, cache_mla_ckv, cache_mla_kr, page_table, c_prompt, c_sample, ada_w, ada_b, norm_attn_g, norm_ffn_g, swa_w_qkv, swa_b_qkv, swa_g_q, swa_g_k, swa_sinks, swa_w_o, swa_b_o, mla_w_dkv, mla_g_qa, mla_g_kva, mla_w_uq, mla_g_q, mla_w_uk, mla_g_k, mla_w_uv, mla_w_o, moe_w_router, moe_b_router, moe_w_gate_up, moe_b_gate_up, moe_w_down, moe_b_down):
    b, s, d = x_prompt.shape
    db, qt, _ = x_sample.shape
    depth = ada_w.shape[0]
    past_len = page_table.shape[1] * cache_mla_ckv.shape[2]
    geom = _Geom(b, s, db, qt, d)

    n_c = b + db
    m_pad = -(-n_c // 8) * 8
    c_all = jnp.pad(jnp.concatenate([c_prompt, c_sample], axis=0), ((0, m_pad - n_c), (0, 0)))
    mod = _adaln(c_all, ada_w, ada_b)

    x = jnp.concatenate([x_prompt.reshape(b * s, d), x_sample.reshape(db * qt, d)], axis=0)
    swa_kp, swa_vp, swa_ks, swa_vs = [], [], [], []
    ckv_p, kr_p, ckv_s, kr_s = [], [], [], []
    n_mixers = 2
    for i in range(depth):
        modx = geom.expand_mod(mod[i])
        g_attn = norm_attn_g[i].reshape(1, d)
        g_ffn = norm_ffn_g[i].reshape(1, d)
        j = i // n_mixers
        if i % n_mixers == 0:
            x, kp, vp, ks, vs = _swa_layer(geom, x, modx, g_attn, swa_w_qkv[j], swa_b_qkv[j], swa_g_q[j], swa_g_k[j],
                                           swa_sinks[j], swa_w_o[j], swa_b_o[j], state_swa_k[j], state_swa_v[j], past_len)
            swa_kp.append(kp)
            swa_vp.append(vp)
            swa_ks.append(ks)
            swa_vs.append(vs)
        else:
            x, cp, rp, cs, rs = _mla_layer(geom, j, x, modx, g_attn, mla_w_dkv[j], mla_g_qa[j], mla_g_kva[j], mla_w_uq[j],
                                           mla_g_q[j], mla_w_uk[j], mla_g_k[j], mla_w_uv[j], mla_w_o[j],
                                           cache_mla_ckv, cache_mla_kr, page_table, past_len)
            ckv_p.append(cp)
            kr_p.append(rp)
            ckv_s.append(cs)
            kr_s.append(rs)
        x = _moe_layer(geom, i, x, modx, g_ffn, moe_w_router[i], moe_b_router[i], moe_w_gate_up, moe_b_gate_up,
                       moe_w_down, moe_b_down)

    y_prompt = x[:b * s].reshape(b, s, d)
    y_sample = x[b * s:].reshape(db, qt, d)
    return (y_prompt, y_sample, jnp.stack(swa_kp), jnp.stack(swa_vp), jnp.stack(swa_ks), jnp.stack(swa_vs),
            jnp.stack(ckv_p), jnp.stack(kr_p), jnp.stack(ckv_s), jnp.stack(kr_s))
```
